```python
import jax, jax.numpy as jnp
from jax import lax
import numpy as np

D_MODEL = 1024
BATCH = 4
SEQ = 4096
DEPTH = 2
DEC_BATCH = 128
DEC_SEQ = 1
PAST_LEN = 16384
PAGE_SIZE = 128

HEAD_DIM = 64
N_BRANCH = 4
BRANCH_W = D_MODEL // 4
RWKV_HEADS = BRANCH_W // HEAD_DIM
DECAY_LORA = 64
ICLR_LORA = 64
GATE_LORA = 128
RWKV_IN = 3 * BRANCH_W + DECAY_LORA + ICLR_LORA + GATE_LORA
RWKV_SPLITS = [BRANCH_W, 2 * BRANCH_W, 3 * BRANCH_W, 3 * BRANCH_W + DECAY_LORA, 3 * BRANCH_W + DECAY_LORA + ICLR_LORA]
DECAY_SCALE = 0.6065306597126334
GN_EPS = 64e-5
RET_HEADS = BRANCH_W // HEAD_DIM
RET_IN = 4 * BRANCH_W
RET_CHUNK = 128
MLA_HEADS = BRANCH_W // HEAD_DIM
Q_LORA = 3 * BRANCH_W // 4
KV_LORA = BRANCH_W
NOPE_DIM = 64
ROPE_DIM = 32
MLA_V_DIM = 64
MLA_IN = Q_LORA + KV_LORA + ROPE_DIM
MLA_CACHE_W = KV_LORA + ROPE_DIM
MLA_SPLITS = [Q_LORA, Q_LORA + KV_LORA]
MLA_SCALE = (NOPE_DIM + ROPE_DIM) ** -0.5
FOX_HEADS = BRANCH_W // HEAD_DIM
FOX_KV_HEADS = FOX_HEADS // 2
FOX_REP = FOX_HEADS // FOX_KV_HEADS
FOX_IN = FOX_HEADS * HEAD_DIM + 2 * FOX_KV_HEADS * HEAD_DIM + FOX_HEADS
FOX_SPLITS = [FOX_HEADS * HEAD_DIM, FOX_HEADS * HEAD_DIM + FOX_KV_HEADS * HEAD_DIM, FOX_HEADS * HEAD_DIM + 2 * FOX_KV_HEADS * HEAD_DIM]
FOX_SCALE = HEAD_DIM ** -0.5
FORGET_BIAS = 3.0
GATE_IN = N_BRANCH * D_MODEL
IN_W = RWKV_IN + RET_IN + MLA_IN + FOX_IN + GATE_IN
IN_SPLITS = [RWKV_IN, RWKV_IN + RET_IN, RWKV_IN + RET_IN + MLA_IN, RWKV_IN + RET_IN + MLA_IN + FOX_IN]
D_FF = 4 * D_MODEL
Q_BLOCK = 128
ROPE_BASE = 10000.0
NORM_EPS = 1e-6
NEG = -1e30

kernel_name = 'hybrid_rwkv7_retnet_mla_fox_step'


def rms_norm(x, g, eps=NORM_EPS):
    xf = x.astype(jnp.float32)
    y = xf * lax.rsqrt(jnp.mean(xf * xf, axis=-1, keepdims=True) + eps)
    return (y * g.astype(jnp.float32)).astype(x.dtype)


def rotary(x, pos):
    half = x.shape[-1] // 2
    inv = ROPE_BASE ** (-jnp.arange(half, dtype=jnp.float32) / half)
    ang = pos.astype(jnp.float32)[:, None] * inv[None, :]
    cos, sin = jnp.cos(ang)[:, None, :], jnp.sin(ang)[:, None, :]
    xf = x.astype(jnp.float32)
    x1, x2 = xf[..., :half], xf[..., half:]
    return jnp.concatenate([x1 * cos - x2 * sin, x2 * cos + x1 * sin], axis=-1).astype(x.dtype)


def causal_mask(n):
    return jnp.tril(jnp.ones((n, n), dtype=bool))


def joint_softmax(s_past, s_new):
    p = jax.nn.softmax(jnp.concatenate([s_past, s_new], axis=-1), axis=-1)
    t = s_past.shape[-1]
    return p[..., :t], p[..., t:]


def causal_blocks(fn, *q_arrays):
    n, length = q_arrays[0].shape[:2]
    nb = length // Q_BLOCK
    blocks = tuple(jnp.moveaxis(a.reshape(n, nb, Q_BLOCK, *a.shape[2:]), 1, 0) for a in q_arrays)
    out = lax.map(lambda args: fn(args[0], *args[1:]), (jnp.arange(nb),) + blocks)
    out = jnp.moveaxis(out, 0, 1)
    return out.reshape(n, length, *out.shape[3:])


def rwkv7_mix(pa, shift0, s0, lw):
    n, length, _ = pa.shape
    f32 = jnp.float32
    prev = jnp.concatenate([shift0[:, None, :].astype(pa.dtype), pa[:, :-1]], axis=1)
    xm = pa + (prev - pa) * lw['rwkv_mu']
    r, k, v, wl, al, gl = jnp.split(xm, RWKV_SPLITS, axis=-1)
    logw = -DECAY_SCALE * jax.nn.sigmoid((lw['rwkv_w0'] + jnp.tanh(wl) @ lw['rwkv_w2']).astype(f32))
    a = jax.nn.sigmoid((lw['rwkv_a0'] + al @ lw['rwkv_a2']).astype(f32))
    g = jax.nn.sigmoid(gl) @ lw['rwkv_g2']
    heads = lambda t: t.astype(f32).reshape(n, length, RWKV_HEADS, HEAD_DIM)
    kh, a, r, v, w = heads(k), heads(a), heads(r), heads(v), heads(jnp.exp(logw))
    kk = kh * lw['rwkv_kk'].astype(f32)
    kk = kk / jnp.maximum(jnp.sqrt(jnp.sum(kk * kk, axis=-1, keepdims=True)), 1e-12)
    kh = kh * (1.0 + (a - 1.0) * lw['rwkv_ka'].astype(f32))

    def step(s, xs):
        r_t, w_t, k_t, v_t, kk_t, a_t = xs
        sa = jnp.einsum('nhvk,nhk->nhv', s, -kk_t)
        s = s * w_t[:, :, None, :] + sa[..., None] * (kk_t * a_t)[:, :, None, :] + v_t[..., None] * k_t[:, :, None, :]
        return s, jnp.einsum('nhvk,nhk->nhv', s, r_t)

    xs = tuple(jnp.moveaxis(t, 1, 0) for t in (r, w, kh, v, kk, a))
    s, y = lax.scan(step, s0.astype(f32), xs)
    y = jnp.moveaxis(y, 0, 1)
    y = y + jnp.sum(r * kh * lw['rwkv_rk'].astype(f32), axis=-1, keepdims=True) * v
    mu = jnp.mean(y, axis=-1, keepdims=True)
    var = jnp.mean(jnp.square(y - mu), axis=-1, keepdims=True)
    y = ((y - mu) * lax.rsqrt(var + GN_EPS)).reshape(n, length, BRANCH_W) * lw['rwkv_ln_g'] + lw['rwkv_ln_b']
    return (y * g).astype(pa.dtype), s, pa[:, -1]


def retention_log_decay():
    return jnp.log1p(-jnp.exp2(-5.0 - jnp.arange(RET_HEADS, dtype=jnp.float32)))


def retention_project(pb, pos):
    n, length, _ = pb.shape
    q, k, v, g = jnp.split(pb, 4, axis=-1)
    heads = lambda t: t.reshape(n, length, RET_HEADS, HEAD_DIM)
    q = rotary(heads(q), pos)
    k = rotary(heads(k), pos) * (HEAD_DIM ** -0.5)
    return q, k, heads(v), g


def retention_chunk(s, q, k, v):
    f32 = jnp.float32
    q, k, v = q.astype(f32), k.astype(f32), v.astype(f32)
    length = q.shape[1]
    lg = retention_log_decay()
    t = jnp.arange(length, dtype=f32)
    diff = t[:, None] - t[None, :]
    dmask = jnp.where(diff >= 0, jnp.exp(lg[:, None, None] * jnp.maximum(diff, 0.0)), 0.0)
    qk = jnp.einsum('nthd,nshd->nhts', q, k) * dmask
    o = jnp.einsum('nhts,nshe->nthe', qk, v)
    o = o + jnp.einsum('nthd,nhde->nthe', q * jnp.exp(lg[None, :] * (t[:, None] + 1.0))[None, :, :, None], s)
    s = s * jnp.exp(lg * length)[None, :, None, None] + jnp.einsum(
        'nshd,nshe->nhde', k * jnp.exp(lg[None, :] * (length - 1.0 - t[:, None]))[None, :, :, None], v)
    return s, o


def retention_chunked(s0, q, k, v):
    n, length = q.shape[:2]
    nc = length // RET_CHUNK
    to_chunks = lambda t: jnp.moveaxis(t.reshape(n, nc, RET_CHUNK, RET_HEADS, HEAD_DIM), 1, 0)
    s, o = lax.scan(lambda s, xs: retention_chunk(s, *xs), s0, (to_chunks(q), to_chunks(k), to_chunks(v)))
    return s, jnp.moveaxis(o, 0, 1).reshape(n, length, RET_HEADS, HEAD_DIM)


def retention_out(o, g):
    n, length = o.shape[:2]
    o = o * lax.rsqrt(jnp.mean(o * o, axis=-1, keepdims=True) + NORM_EPS)
    return (jax.nn.silu(g.astype(jnp.float32)) * o.reshape(n, length, BRANCH_W)).astype(g.dtype)


def mla_project(pc, pos, lw):
    cq, ckv, kr = jnp.split(pc, MLA_SPLITS, axis=-1)
    q = jnp.einsum('nlc,chd->nlhd', rms_norm(cq, lw['mla_qn_g']), lw['mla_wuq']) * MLA_SCALE
    q_lat = jnp.einsum('nlhd,chd->nlhc', q[..., :NOPE_DIM], lw['mla_wuk'])
    q_rope = rotary(q[..., NOPE_DIM:], pos)
    ckv = rms_norm(ckv, lw['mla_kvn_g'])
    kr = rotary(kr[:, :, None, :], pos)[:, :, 0]
    qf = jnp.concatenate([q_lat, q_rope], axis=-1)
    rows = jnp.concatenate([ckv, kr], axis=-1)
    return qf, rows


def mla_attend_prompt(qf, rows):
    length = rows.shape[1]
    kpos = jnp.arange(length)

    def blk(b, qb):
        qpos = b * Q_BLOCK + jnp.arange(Q_BLOCK)
        s = jnp.einsum('nqhc,nkc->nhqk', qb, rows).astype(jnp.float32)
        s = jnp.where(kpos[None, :] <= qpos[:, None], s, NEG)
        p = jax.nn.softmax(s, axis=-1).astype(rows.dtype)
        return jnp.einsum('nhqk,nkc->nqhc', p, rows)[..., :KV_LORA]
    return causal_blocks(blk, qf)


def mla_attend_cached(past):
    def attend(qf, rows):
        length = rows.shape[1]
        s_past = jnp.einsum('nqhc,nkc->nhqk', qf, past).astype(jnp.float32)
        s_new = jnp.where(causal_mask(length), jnp.einsum('nqhc,nkc->nhqk', qf, rows).astype(jnp.float32), NEG)
        p_past, p_new = joint_softmax(s_past, s_new)
        o = (jnp.einsum('nhqk,nkc->nqhc', p_past.astype(past.dtype), past)
             + jnp.einsum('nhqk,nkc->nqhc', p_new.astype(rows.dtype), rows))
        return o[..., :KV_LORA]
    return attend


def mla_values(o_lat, wuv):
    n, length = o_lat.shape[:2]
    return jnp.einsum('nqhc,chd->nqhd', o_lat, wuv).reshape(n, length, BRANCH_W)


def fox_project(pd, bf):
    n, length, _ = pd.shape
    q, k, v, fl = jnp.split(pd, FOX_SPLITS, axis=-1)
    q = q.reshape(n, length, FOX_KV_HEADS, FOX_REP, HEAD_DIM) * FOX_SCALE
    k = k.reshape(n, length, FOX_KV_HEADS, HEAD_DIM)
    v = v.reshape(n, length, FOX_KV_HEADS, HEAD_DIM)
    logf = jax.nn.log_sigmoid((fl + bf).astype(jnp.float32))
    return q, k, v, logf


def fox_attend_prompt(q, k, v, logf):
    n, length = q.shape[:2]
    c = jnp.cumsum(logf, axis=1).reshape(n, length, FOX_KV_HEADS, FOX_REP)
    ck = jnp.moveaxis(c, 1, -1)[..., None, :]
    kpos = jnp.arange(length)

    def blk(b, qb, cqb):
        qpos = b * Q_BLOCK + jnp.arange(Q_BLOCK)
        s = jnp.einsum('nqgrd,nkgd->ngrqk', qb, k).astype(jnp.float32) + (jnp.moveaxis(cqb, 1, -1)[..., :, None] - ck)
        s = jnp.where(kpos[None, :] <= qpos[:, None], s, NEG)
        p = jax.nn.softmax(s, axis=-1).astype(v.dtype)
        return jnp.einsum('ngrqk,nkgd->nqgrd', p, v)
    return causal_blocks(blk, q, c)


def fox_attend_cached(k_past, v_past, logf_past):
    n, t_len = logf_past.shape[:2]
    lp = logf_past.astype(jnp.float32)
    suffix = lax.cumsum(lp, axis=1, reverse=True) - lp
    suffix = jnp.moveaxis(suffix.reshape(n, t_len, FOX_KV_HEADS, FOX_REP), 1, -1)[..., None, :]

    def attend(q, k, v, logf):
        length = q.shape[1]
        cn = jnp.moveaxis(jnp.cumsum(logf, axis=1).reshape(n, length, FOX_KV_HEADS, FOX_REP), 1, -1)
        s_past = jnp.einsum('nqgrd,nkgd->ngrqk', q, k_past).astype(jnp.float32) + cn[..., :, None] + suffix
        s_new = jnp.einsum('nqgrd,nkgd->ngrqk', q, k).astype(jnp.float32) + cn[..., :, None] - cn[..., None, :]
        s_new = jnp.where(causal_mask(length), s_new, NEG)
        p_past, p_new = joint_softmax(s_past, s_new)
        return (jnp.einsum('ngrqk,nkgd->nqgrd', p_past.astype(v_past.dtype), v_past)
                + jnp.einsum('ngrqk,nkgd->nqgrd', p_new.astype(v.dtype), v))
    return attend


def layer_forward(x, pos, lw, shift0, rwkv0, ret0, retention_fn, mla_attend, fox_attend):
    n, length, _ = x.shape
    h = rms_norm(x, lw['norm1_g'])
    pa, pb, pc, pd, pg = jnp.split(h @ lw['w_in'], IN_SPLITS, axis=-1)
    ya, rwkv_new, shift_new = rwkv7_mix(pa, shift0, rwkv0, lw)
    q, k, v, g = retention_project(pb, pos)
    ret_new, o = retention_fn(ret0, q, k, v)
    yb = retention_out(o, g)
    qf, mla_rows = mla_project(pc, pos, lw)
    yc = mla_values(mla_attend(qf, mla_rows), lw['mla_wuv'])
    fq, fk, fv, flogf = fox_project(pd, lw['fox_bf'])
    yd = fox_attend(fq, fk, fv, flogf).reshape(n, length, BRANCH_W)
    branches = jnp.stack([ya, yb.astype(ya.dtype), yc.astype(ya.dtype), yd.astype(ya.dtype)], axis=2)
    gates = jax.nn.sigmoid(pg.reshape(n, length, N_BRANCH, D_MODEL))
    merged = jnp.sum(gates * jnp.einsum('nlbc,bcd->nlbd', branches, lw['w_branch']), axis=2)
    x = x + merged @ lw['w_out']
    hf = rms_norm(x, lw['norm2_g'])
    x = x + jnp.square(jax.nn.relu(hf @ lw['w_up'])) @ lw['w_down']
    return x, (mla_rows, fk, fv, flogf, rwkv_new, shift_new, ret_new)


def setup_inputs(seed: int = 0) -> dict:
    key = jax.random.key(seed)
    keys = iter(jax.random.split(key, 48))
    f32 = jnp.float32

    def nrm(shape, scale):
        return jax.random.normal(next(keys), shape, f32) * scale

    def gain(shape):
        return 1.0 + nrm(shape, 0.02)

    n_pages = PAST_LEN // PAGE_SIZE
    n_used = DEC_BATCH * n_pages
    n_pool = n_used + (n_used + 3) // 4
    page_table = jax.random.permutation(next(keys), n_pool)[:n_used].reshape(DEC_BATCH, n_pages).astype(jnp.int32)
    return {
        'x_prompt': nrm((BATCH, SEQ, D_MODEL), 1.0),
        'x_sample': nrm((DEC_BATCH, DEC_SEQ, D_MODEL), 1.0),
        'cache_mla': nrm((DEPTH, n_pool, PAGE_SIZE, MLA_CACHE_W), 1.0),
        'cache_fox_k': nrm((DEPTH, n_pool, PAGE_SIZE, FOX_KV_HEADS, HEAD_DIM), 1.0),
        'cache_fox_v': nrm((DEPTH, n_pool, PAGE_SIZE, FOX_KV_HEADS, HEAD_DIM), 1.0),
        'cache_fox_logf': jax.nn.log_sigmoid(FORGET_BIAS + nrm((DEPTH, n_pool, PAGE_SIZE, FOX_HEADS), 1.0)),
        'state_rwkv': nrm((DEPTH, DEC_BATCH, RWKV_HEADS, HEAD_DIM, HEAD_DIM), 0.5),
        'state_rwkv_shift': nrm((DEPTH, DEC_BATCH, RWKV_IN), 1.0),
        'state_ret': nrm((DEPTH, DEC_BATCH, RET_HEADS, HEAD_DIM, HEAD_DIM), 1.0),
        'page_table': page_table,
        'norm1_g': gain((DEPTH, D_MODEL)),
        'norm2_g': gain((DEPTH, D_MODEL)),
        'final_g': gain((D_MODEL,)),
        'w_in': nrm((DEPTH, D_MODEL, IN_W), D_MODEL ** -0.5),
        'rwkv_mu': jax.random.uniform(next(keys), (DEPTH, RWKV_IN), f32),
        'rwkv_w0': nrm((DEPTH, BRANCH_W), 0.5),
        'rwkv_w2': nrm((DEPTH, DECAY_LORA, BRANCH_W), 0.1),
        'rwkv_a0': nrm((DEPTH, BRANCH_W), 0.5),
        'rwkv_a2': nrm((DEPTH, ICLR_LORA, BRANCH_W), 0.1),
        'rwkv_g2': nrm((DEPTH, GATE_LORA, BRANCH_W), GATE_LORA ** -0.5),
        'rwkv_kk': 0.85 + nrm((DEPTH, RWKV_HEADS, HEAD_DIM), 0.05),
        'rwkv_ka': 1.0 + nrm((DEPTH, RWKV_HEADS, HEAD_DIM), 0.05),
        'rwkv_rk': nrm((DEPTH, RWKV_HEADS, HEAD_DIM), 0.1),
        'rwkv_ln_g': gain((DEPTH, BRANCH_W)),
        'rwkv_ln_b': nrm((DEPTH, BRANCH_W), 0.01),
        'mla_qn_g': gain((DEPTH, Q_LORA)),
        'mla_kvn_g': gain((DEPTH, KV_LORA)),
        'mla_wuq': nrm((DEPTH, Q_LORA, MLA_HEADS, NOPE_DIM + ROPE_DIM), Q_LORA ** -0.5),
        'mla_wuk': nrm((DEPTH, KV_LORA, MLA_HEADS, NOPE_DIM), KV_LORA ** -0.5),
        'mla_wuv': nrm((DEPTH, KV_LORA, MLA_HEADS, MLA_V_DIM), KV_LORA ** -0.5),
        'fox_bf': FORGET_BIAS + nrm((DEPTH, FOX_HEADS), 0.1),
        'w_branch': nrm((DEPTH, N_BRANCH, BRANCH_W, D_MODEL), BRANCH_W ** -0.5),
        'w_out': nrm((DEPTH, D_MODEL, D_MODEL), D_MODEL ** -0.5),
        'w_up': nrm((DEPTH, D_MODEL, D_FF), D_MODEL ** -0.5),
        'w_down': nrm((DEPTH, D_FF, D_MODEL), D_FF ** -0.5),
    }


def reference(x_prompt, x_sample, cache_mla, cache_fox_k, cache_fox_v, cache_fox_logf, state_rwkv,
              state_rwkv_shift, state_ret, page_table, norm1_g, norm2_g, final_g, w_in, rwkv_mu, rwkv_w0,
              rwkv_w2, rwkv_a0, rwkv_a2, rwkv_g2, rwkv_kk, rwkv_ka, rwkv_rk, rwkv_ln_g, rwkv_ln_b, mla_qn_g,
              mla_kvn_g, mla_wuq, mla_wuk, mla_wuv, fox_bf, w_branch, w_out, w_up, w_down):
    b, lp = x_prompt.shape[:2]
    db, ls = x_sample.shape[:2]
    t_past = page_table.shape[1] * PAGE_SIZE
    pos_p = jnp.arange(lp)
    pos_s = t_past + jnp.arange(ls)
    xp, xs = x_prompt, x_sample
    new_p = [[] for _ in range(7)]
    new_s = [[] for _ in range(7)]
    for l in range(DEPTH):
        lw = dict(norm1_g=norm1_g[l], norm2_g=norm2_g[l], w_in=w_in[l], rwkv_mu=rwkv_mu[l],
                  rwkv_w0=rwkv_w0[l], rwkv_w2=rwkv_w2[l], rwkv_a0=rwkv_a0[l], rwkv_a2=rwkv_a2[l],
                  rwkv_g2=rwkv_g2[l], rwkv_kk=rwkv_kk[l], rwkv_ka=rwkv_ka[l], rwkv_rk=rwkv_rk[l],
                  rwkv_ln_g=rwkv_ln_g[l], rwkv_ln_b=rwkv_ln_b[l], mla_qn_g=mla_qn_g[l],
                  mla_kvn_g=mla_kvn_g[l], mla_wuq=mla_wuq[l], mla_wuk=mla_wuk[l], mla_wuv=mla_wuv[l],
                  fox_bf=fox_bf[l], w_branch=w_branch[l], w_out=w_out[l], w_up=w_up[l], w_down=w_down[l])
        xp, st = layer_forward(
            xp, pos_p, lw,
            jnp.zeros((b, RWKV_IN), xp.dtype),
            jnp.zeros((b, RWKV_HEADS, HEAD_DIM, HEAD_DIM), jnp.float32),
            jnp.zeros((b, RET_HEADS, HEAD_DIM, HEAD_DIM), jnp.float32),
            retention_chunked, mla_attend_prompt, fox_attend_prompt)
        for i in range(7):
            new_p[i].append(st[i])
        past_mla = cache_mla[l, page_table].reshape(db, t_past, MLA_CACHE_W)
        k_past = cache_fox_k[l, page_table].reshape(db, t_past, FOX_KV_HEADS, HEAD_DIM)
        v_past = cache_fox_v[l, page_table].reshape(db, t_past, FOX_KV_HEADS, HEAD_DIM)
        lf_past = cache_fox_logf[l, page_table].reshape(db, t_past, FOX_HEADS)
        xs, st = layer_forward(
            xs, pos_s, lw,
            state_rwkv_shift[l], state_rwkv[l].astype(jnp.float32), state_ret[l].astype(jnp.float32),
            retention_chunk, mla_attend_cached(past_mla), fox_attend_cached(k_past, v_past, lf_past))
        for i in range(7):
            new_s[i].append(st[i])
    y_prompt = rms_norm(xp, final_g)
    y_sample = rms_norm(xs, final_g)
    sp = [jnp.stack(a) for a in new_p]
    ss = [jnp.stack(a) for a in new_s]
    return (y_prompt, y_sample, sp[0], ss[0], sp[1], ss[1], sp[2], ss[2], sp[3], ss[3],
            sp[4], ss[4], sp[5], ss[5], sp[6], ss[6])
```

```python
import functools

import jax
import jax.numpy as jnp
import numpy as np
from jax import lax
from jax.experimental import pallas as pl
from jax.experimental.pallas import tpu as pltpu

F32 = jnp.float32
BF16 = jnp.bfloat16

D_MODEL = 1024
HEAD_DIM = 64
N_HEADS = 4
BRANCH_W = 256
PAGE = 128
DECAY_LORA = 64
ICLR_LORA = 64
GATE_LORA = 128
RWKV_IN = 1024
DECAY_SCALE = 0.6065306597126334
GN_EPS = 64e-5
RET_CHUNK = 128
Q_LORA = 192
KV_LORA = 256
NOPE_DIM = 64
ROPE_DIM = 32
MLA_W = KV_LORA + ROPE_DIM
MLA_WP = 384
MLA_SCALE = (NOPE_DIM + ROPE_DIM) ** -0.5
FOX_SCALE = HEAD_DIM ** -0.5
D_FF = 4096
ROPE_BASE = 10000.0
NORM_EPS = 1e-6
NEG = -1e30

P_GATE = 0
P_RWKV = 4096
P_RET = 5120
P_FQ = 6144
P_FK = 6400
P_FV = 6528
P_CQ = 6656
P_CKV = 6912
P_KRFL = 7168
P_W = 7296
P_TN = 2432

VMEM_LIMIT = 56 * 1024 * 1024


def _cparams(sem):
    return pltpu.CompilerParams(dimension_semantics=sem, vmem_limit_bytes=VMEM_LIMIT)


def _bdot(a, b):
    return jnp.dot(a.astype(BF16), b.astype(BF16), preferred_element_type=F32)


def _bdot_nt(a, b):
    return lax.dot_general(a.astype(BF16), b.astype(BF16), (((1,), (1,)), ((), ())),
                           preferred_element_type=F32)


def _bdot_tn(a, b):
    return lax.dot_general(a.astype(BF16), b.astype(BF16), (((0,), (0,)), ((), ())),
                           preferred_element_type=F32)


def _split3(a):
    a1 = a.astype(BF16)
    r1 = a - a1.astype(F32)
    a2 = r1.astype(BF16)
    a3 = (r1 - a2.astype(F32)).astype(BF16)
    return a1, a2, a3


def _dot01(a, ones01):
    a1, a2, a3 = _split3(a)
    d = lambda p: jnp.dot(p, ones01, preferred_element_type=F32)
    return d(a1) + (d(a2) + d(a3))


def _dot01_left(ones01, a):
    a1, a2, a3 = _split3(a)
    d = lambda p: jnp.dot(ones01, p, preferred_element_type=F32)
    return d(a1) + (d(a2) + d(a3))


def _sigmoid(x):
    return 1.0 / (1.0 + jnp.exp(-x))


def _rms(x, g, n=None):
    n = x.shape[-1] if n is None else n
    ms = jnp.sum(x * x, axis=-1, keepdims=True) * (1.0 / n)
    return x * lax.rsqrt(ms + NORM_EPS) * g


def _rotate(x, cos, sin_signed, half, width):
    lane = lax.broadcasted_iota(jnp.int32, x.shape, x.ndim - 1)
    first = (lane % (2 * half)) < half
    partner = jnp.where(first, pltpu.roll(x, width - half, x.ndim - 1), pltpu.roll(x, half, x.ndim - 1))
    return x * cos + partner * sin_signed


def _norm_matmul_kernel(x_ref, g_ref, w_ref, o_ref):
    h = _rms(x_ref[...], g_ref[...])
    o_ref[...] = jnp.dot(h.astype(BF16), w_ref[...], preferred_element_type=F32)


def _norm_matmul(x, g, w, tm):
    rows = x.shape[0]
    return pl.pallas_call(
        _norm_matmul_kernel,
        grid=(P_W // P_TN, rows // tm),
        in_specs=[pl.BlockSpec((tm, D_MODEL), lambda j, i: (i, 0)),
                  pl.BlockSpec((1, D_MODEL), lambda j, i: (0, 0)),
                  pl.BlockSpec((D_MODEL, P_TN), lambda j, i: (0, j))],
        out_specs=pl.BlockSpec((tm, P_TN), lambda j, i: (i, j)),
        out_shape=jax.ShapeDtypeStruct((rows, P_W), F32),
        compiler_params=_cparams(("parallel", "parallel")),
        name="norm_in_proj",
    )(x, g, w)


def _rwkv_prep_kernel(pa_ref, prev_ref, mu_ref, w0_ref, w2_ref, a0_ref, a2_ref, g2_ref, kkp_ref,
                      ka_ref, bd_ref, r_o, w_o, kh_o, v_o, kk_o, b_o, g_o, *, shifted, tiles_per_seq):
    pa = pa_ref[...]
    if shifted:
        i = pl.program_id(0)
        rolled = pltpu.roll(pa, 1, 0)
        halo = prev_ref[7:8, :]
        halo = jnp.where(i % tiles_per_seq == 0, jnp.zeros_like(halo), halo)
        row = lax.broadcasted_iota(jnp.int32, pa.shape, 0)
        prev = jnp.where(row == 0, halo, rolled)
    else:
        prev = prev_ref[...]
    xm = pa + (prev - pa) * mu_ref[...]
    r = xm[:, 0:256]
    k = xm[:, 256:512]
    v = xm[:, 512:768]
    wl = xm[:, 768:832]
    al = xm[:, 832:896]
    gl = xm[:, 896:1024]
    logw = -DECAY_SCALE * _sigmoid(w0_ref[...] + _bdot(jnp.tanh(wl), w2_ref[...]))
    a = _sigmoid(a0_ref[...] + _bdot(al, a2_ref[...]))
    g = _bdot(_sigmoid(gl), g2_ref[...])
    kk = k * kkp_ref[...]
    ss = _dot01(kk * kk, bd_ref[...])
    kk = kk / jnp.maximum(jnp.sqrt(ss), 1e-12)
    kh = k * (1.0 + (a - 1.0) * ka_ref[...])
    b = kk * a
    w = jnp.exp(logw)
    for h in range(N_HEADS):
        sl = slice(h * HEAD_DIM, (h + 1) * HEAD_DIM)
        r_o[h] = r[:, sl]
        w_o[h] = w[:, sl]
        kh_o[h] = kh[:, sl]
        v_o[h] = v[:, sl]
        kk_o[h] = kk[:, sl]
        b_o[h] = b[:, sl]
    g_o[...] = g


def _rwkv_prep(p, prev, lw, bd, tm, shifted, seq_len):
    rows = p.shape[0]
    tiles_per_seq = max(seq_len // tm, 1)
    col = P_RWKV // RWKV_IN
    if shifted:
        prev_arr = p
        prev_spec = pl.BlockSpec((8, RWKV_IN), lambda i: (jnp.maximum(i * (tm // 8) - 1, 0), col))
    else:
        prev_arr = prev
        prev_spec = pl.BlockSpec((tm, RWKV_IN), lambda i: (i, 0))
    vec = lambda n: pl.BlockSpec((1, n), lambda i: (0, 0))
    mat = lambda a, b: pl.BlockSpec((a, b), lambda i: (0, 0))
    hm = jax.ShapeDtypeStruct((N_HEADS, rows, HEAD_DIM), F32)
    hm_spec = pl.BlockSpec((N_HEADS, tm, HEAD_DIM), lambda i: (0, i, 0))
    return pl.pallas_call(
        functools.partial(_rwkv_prep_kernel, shifted=shifted, tiles_per_seq=tiles_per_seq),
        grid=(rows // tm,),
        in_specs=[pl.BlockSpec((tm, RWKV_IN), lambda i: (i, col)), prev_spec,
                  vec(RWKV_IN), vec(256), mat(DECAY_LORA, 256), vec(256), mat(ICLR_LORA, 256),
                  mat(GATE_LORA, 256), vec(256), vec(256), mat(256, 256)],
        out_specs=[hm_spec] * 6 + [pl.BlockSpec((tm, 256), lambda i: (i, 0))],
        out_shape=[hm] * 6 + [jax.ShapeDtypeStruct((rows, 256), F32)],
        compiler_params=_cparams(("parallel",)),
        name="rwkv_prep",
    )(p, prev_arr, lw["mu"], lw["w0"], lw["w2"], lw["a0"], lw["a2"], lw["g2"], lw["kkp"], lw["ka"], bd)


def _rwkv_step(s, r, w, kh, v, kk, b, eye):
    sa = jnp.sum(s * kk, axis=-1, keepdims=True)
    v_col = jnp.sum(jnp.where(eye, v, 0.0), axis=-1, keepdims=True)
    s = s * w - sa * b + v_col * kh
    y_col = jnp.sum(s * r, axis=-1, keepdims=True)
    y_row = jnp.sum(jnp.where(eye, y_col, 0.0), axis=0, keepdims=True)
    return s, y_row


def _eye64():
    return (lax.broadcasted_iota(jnp.int32, (HEAD_DIM, HEAD_DIM), 0)
            == lax.broadcasted_iota(jnp.int32, (HEAD_DIM, HEAD_DIM), 1))


def _rwkv_scan_kernel(r_ref, w_ref, kh_ref, v_ref, kk_ref, b_ref, y_ref, sfin_ref, s_scr, *, nseq, tc):
    c = pl.program_id(0)

    @pl.when(c == 0)
    def _():
        s_scr[...] = jnp.zeros_like(s_scr)

    eye = _eye64()

    def body(t, carry):
        for n in range(nseq):
            for h in range(N_HEADS):
                row = lambda ref: ref[h, n, pl.ds(t, 1), :]
                s, y_row = _rwkv_step(s_scr[n, h], row(r_ref), row(w_ref), row(kh_ref), row(v_ref),
                                      row(kk_ref), row(b_ref), eye)
                s_scr[n, h] = s
                y_ref[h, n, pl.ds(t, 1), :] = y_row
        return carry

    lax.fori_loop(0, tc, body, 0)

    @pl.when(c == pl.num_programs(0) - 1)
    def _():
        sfin_ref[...] = s_scr[...]


def _rwkv_scan(r, w, kh, v, kk, b, nseq, seq_len, tc):
    view = lambda a: a.reshape(N_HEADS, nseq, seq_len, HEAD_DIM)
    spec = pl.BlockSpec((N_HEADS, nseq, tc, HEAD_DIM), lambda c: (0, 0, c, 0))
    y, sfin = pl.pallas_call(
        functools.partial(_rwkv_scan_kernel, nseq=nseq, tc=tc),
        grid=(seq_len // tc,),
        in_specs=[spec] * 6,
        out_specs=[spec, pl.BlockSpec((nseq, N_HEADS, HEAD_DIM, HEAD_DIM), lambda c: (0, 0, 0, 0))],
        out_shape=[jax.ShapeDtypeStruct((N_HEADS, nseq, seq_len, HEAD_DIM), F32),
                   jax.ShapeDtypeStruct((nseq, N_HEADS, HEAD_DIM, HEAD_DIM), F32)],
        scratch_shapes=[pltpu.VMEM((nseq, N_HEADS, HEAD_DIM, HEAD_DIM), F32)],
        compiler_params=_cparams(("arbitrary",)),
        name="rwkv_scan",
    )(view(r), view(w), view(kh), view(v), view(kk), view(b))
    return y.reshape(N_HEADS, nseq * seq_len, HEAD_DIM), sfin


def _rwkv_single_kernel(r_ref, w_ref, kh_ref, v_ref, kk_ref, b_ref, s_ref, y_ref, so_ref, *, bs):
    eye = _eye64()
    for n in range(bs):
        for h in range(N_HEADS):
            row = lambda ref: ref[h, n:n + 1, :]
            s, y_row = _rwkv_step(s_ref[n, h], row(r_ref), row(w_ref), row(kh_ref), row(v_ref),
                                  row(kk_ref), row(b_ref), eye)
            so_ref[n, h] = s
            y_ref[h, n:n + 1, :] = y_row


def _rwkv_single(r, w, kh, v, kk, b, s0, bs=8):
    nb = s0.shape[0]
    spec = pl.BlockSpec((N_HEADS, bs, HEAD_DIM), lambda i: (0, i, 0))
    s_spec = pl.BlockSpec((bs, N_HEADS, HEAD_DIM, HEAD_DIM), lambda i: (i, 0, 0, 0))
    return pl.pallas_call(
        functools.partial(_rwkv_single_kernel, bs=bs),
        grid=(nb // bs,),
        in_specs=[spec] * 6 + [s_spec],
        out_specs=[spec, s_spec],
        out_shape=[jax.ShapeDtypeStruct((N_HEADS, nb, HEAD_DIM), F32),
                   jax.ShapeDtypeStruct(s0.shape, F32)],
        compiler_params=_cparams(("parallel",)),
        name="rwkv_single",
    )(r, w, kh, v, kk, b, s0)


def _rwkv_post_kernel(y_ref, r_ref, kh_ref, v_ref, g_ref, rk_ref, lng_ref, lnb_ref, o_ref):
    outs = []
    for h in range(N_HEADS):
        bonus = jnp.sum(r_ref[h] * kh_ref[h] * rk_ref[h], axis=-1, keepdims=True)
        y = y_ref[h] + bonus * v_ref[h]
        mu = jnp.mean(y, axis=-1, keepdims=True)
        d = y - mu
        var = jnp.mean(d * d, axis=-1, keepdims=True)
        outs.append(d * lax.rsqrt(var + GN_EPS))
    y = jnp.concatenate(outs, axis=-1) * lng_ref[...] + lnb_ref[...]
    o_ref[...] = y * g_ref[...]


def _rwkv_post(y, r, kh, v, g, lw, tm):
    rows = g.shape[0]
    hm_spec = pl.BlockSpec((N_HEADS, tm, HEAD_DIM), lambda i: (0, i, 0))
    vec = pl.BlockSpec((1, 256), lambda i: (0, 0))
    return pl.pallas_call(
        _rwkv_post_kernel,
        grid=(rows // tm,),
        in_specs=[hm_spec] * 4 + [pl.BlockSpec((tm, 256), lambda i: (i, 0)),
                                  pl.BlockSpec((N_HEADS, 1, HEAD_DIM), lambda i: (0, 0, 0)), vec, vec],
        out_specs=pl.BlockSpec((tm, 256), lambda i: (i, 0)),
        out_shape=jax.ShapeDtypeStruct((rows, 256), F32),
        compiler_params=_cparams(("parallel",)),
        name="rwkv_post",
    )(y, r, kh, v, g, lw["rk"], lw["ln_g"], lw["ln_b"])


def _ret_log_decay():
    return jnp.log1p(-jnp.exp2(-5.0 - jnp.arange(N_HEADS, dtype=F32)))


def _rope_tables(pos, half, reps):
    inv = ROPE_BASE ** (-jnp.arange(half, dtype=F32) / half)
    ang = pos.astype(F32)[:, None] * inv[None, :]
    cos, sin = jnp.cos(ang), jnp.sin(ang)
    cos_t = jnp.tile(jnp.concatenate([cos, cos], axis=1), (1, reps))
    sin_t = jnp.tile(jnp.concatenate([-sin, sin], axis=1), (1, reps))
    return cos_t, sin_t


def _ret_norm_gate(outs, g):
    o = jnp.concatenate([x * lax.rsqrt(jnp.mean(x * x, axis=-1, keepdims=True) + NORM_EPS) for x in outs],
                        axis=-1)
    return (g * _sigmoid(g)) * o


def _ret_chunk_kernel(pb_ref, cos_ref, sin_ref, dmask_ref, dq_ref, dk_ref, dc_ref, o_ref, sfin_ref, s_scr):
    c = pl.program_id(1)

    @pl.when(c == 0)
    def _():
        s_scr[...] = jnp.zeros_like(s_scr)

    pb = pb_ref[...]
    cos, sin = cos_ref[...], sin_ref[...]
    q = _rotate(pb[:, 0:256], cos, sin, 32, 256)
    k = _rotate(pb[:, 256:512], cos, sin, 32, 256) * (HEAD_DIM ** -0.5)
    v = pb[:, 512:768]
    qd = q * dq_ref[...]
    kd = k * dk_ref[...]
    dc = dc_ref[...]
    outs = []
    for h in range(N_HEADS):
        sl = slice(h * HEAD_DIM, (h + 1) * HEAD_DIM)
        qk = _bdot_nt(q[:, sl], k[:, sl]) * dmask_ref[h]
        s = s_scr[h]
        outs.append(_bdot(qk, v[:, sl]) + _bdot(qd[:, sl], s))
        s_scr[h] = s * dc[h] + _bdot_tn(kd[:, sl], v[:, sl])
    o_ref[...] = _ret_norm_gate(outs, pb[:, 768:1024])

    @pl.when(c == pl.num_programs(1) - 1)
    def _():
        sfin_ref[...] = s_scr[...]


def _ret_prompt(p, nseq, seq_len):
    lg = _ret_log_decay()
    t = jnp.arange(RET_CHUNK, dtype=F32)
    diff = t[:, None] - t[None, :]
    dmask = jnp.where(diff >= 0, jnp.exp(lg[:, None, None] * jnp.maximum(diff, 0.0)), 0.0)
    lanes = lambda a: jnp.repeat(a, HEAD_DIM, axis=-1)
    dq = lanes(jnp.exp(lg[None, :] * (t[:, None] + 1.0)))
    dk = lanes(jnp.exp(lg[None, :] * (RET_CHUNK - 1.0 - t[:, None])))
    dc = jnp.broadcast_to(jnp.exp(lg * RET_CHUNK)[:, None, None], (N_HEADS, 1, HEAD_DIM))
    cos, sin = _rope_tables(jnp.arange(seq_len), 32, N_HEADS)
    nc = seq_len // RET_CHUNK
    full = lambda *s: pl.BlockSpec(s, lambda n, c: (0,) * len(s))
    return pl.pallas_call(
        _ret_chunk_kernel,
        grid=(nseq, nc),
        in_specs=[pl.BlockSpec((RET_CHUNK, 1024), lambda n, c: (n * nc + c, P_RET // 1024)),
                  pl.BlockSpec((RET_CHUNK, 256), lambda n, c: (c, 0)),
                  pl.BlockSpec((RET_CHUNK, 256), lambda n, c: (c, 0)),
                  full(N_HEADS, RET_CHUNK, RET_CHUNK), full(RET_CHUNK, 256), full(RET_CHUNK, 256),
                  full(N_HEADS, 1, HEAD_DIM)],
        out_specs=[pl.BlockSpec((RET_CHUNK, 256), lambda n, c: (n * nc + c, 0)),
                   pl.BlockSpec((None, N_HEADS, HEAD_DIM, HEAD_DIM), lambda n, c: (n, 0, 0, 0))],
        out_shape=[jax.ShapeDtypeStruct((nseq * seq_len, 256), F32),
                   jax.ShapeDtypeStruct((nseq, N_HEADS, HEAD_DIM, HEAD_DIM), F32)],
        scratch_shapes=[pltpu.VMEM((N_HEADS, HEAD_DIM, HEAD_DIM), F32)],
        compiler_params=_cparams(("parallel", "arbitrary")),
        name="retention_chunks",
    )(p, cos, sin, dmask, dq, dk, dc)


def _ret_single_kernel(pb_ref, cos_ref, sin_ref, dec_ref, s_ref, o_ref, so_ref, *, bs):
    pb = pb_ref[...]
    cos, sin = cos_ref[...], sin_ref[...]
    q = _rotate(pb[:, 0:256], cos, sin, 32, 256)
    k = _rotate(pb[:, 256:512], cos, sin, 32, 256) * (HEAD_DIM ** -0.5)
    v = pb[:, 512:768]
    eye = _eye64()
    dec = dec_ref[...]
    rows = []
    for n in range(bs):
        outs = []
        for h in range(N_HEADS):
            sl = slice(h * HEAD_DIM, (h + 1) * HEAD_DIM)
            qr, kr, vr = q[n:n + 1, sl], k[n:n + 1, sl], v[n:n + 1, sl]
            q_col = jnp.sum(jnp.where(eye, qr, 0.0), axis=-1, keepdims=True)
            k_col = jnp.sum(jnp.where(eye, kr, 0.0), axis=-1, keepdims=True)
            s = s_ref[n, h]
            d = dec[h]
            o = jnp.sum(qr * kr, axis=-1, keepdims=True) * vr + jnp.sum((q_col * d) * s, axis=0, keepdims=True)
            so_ref[n, h] = s * d + k_col * vr
            outs.append(o)
        rows.append(jnp.concatenate(outs, axis=-1))
    o = jnp.concatenate(rows, axis=0)
    g = pb[:, 768:1024]
    o_ref[...] = _ret_norm_gate([o[:, h * HEAD_DIM:(h + 1) * HEAD_DIM] for h in range(N_HEADS)], g)


def _ret_single(p, s0, pos, bs=8):
    nb = s0.shape[0]
    cos, sin = _rope_tables(jnp.full((1,), pos), 32, N_HEADS)
    dec = jnp.broadcast_to(jnp.exp(_ret_log_decay())[:, None, None], (N_HEADS, 1, HEAD_DIM))
    s_spec = pl.BlockSpec((bs, N_HEADS, HEAD_DIM, HEAD_DIM), lambda i: (i, 0, 0, 0))
    return pl.pallas_call(
        functools.partial(_ret_single_kernel, bs=bs),
        grid=(nb // bs,),
        in_specs=[pl.BlockSpec((bs, 1024), lambda i: (i, P_RET // 1024)),
                  pl.BlockSpec((1, 256), lambda i: (0, 0)), pl.BlockSpec((1, 256), lambda i: (0, 0)),
                  pl.BlockSpec((N_HEADS, 1, HEAD_DIM), lambda i: (0, 0, 0)), s_spec],
        out_specs=[pl.BlockSpec((bs, 256), lambda i: (i, 0)), s_spec],
        out_shape=[jax.ShapeDtypeStruct((nb, 256), F32), jax.ShapeDtypeStruct(s0.shape, F32)],
        compiler_params=_cparams(("parallel",)),
        name="retention_single",
    )(p, cos, sin, dec, s0)


def _mla_prep_kernel(cq_ref, ckv_ref, kr_ref, cos_ref, sin_ref, qng_ref, kvg_ref, wn_ref, wr_ref, wuk_ref,
                     qf_ref, rows_ref):
    cqn = _rms(cq_ref[...], qng_ref[...], n=Q_LORA)
    q_nope = _bdot(cqn, wn_ref[...]) * MLA_SCALE
    q_rope = _bdot(cqn, wr_ref[...]) * MLA_SCALE
    q_lat = _bdot(q_nope, wuk_ref[...])
    cos, sin = cos_ref[...], sin_ref[...]
    q_rope = _rotate(q_rope, cos, sin, 16, 128)
    kr = _rotate(kr_ref[...], cos, sin, 16, 128)
    lane = lax.broadcasted_iota(jnp.int32, kr.shape, 1)
    keep = lane < ROPE_DIM
    rows_ref[:, 0:KV_LORA] = _rms(ckv_ref[...], kvg_ref[...])
    rows_ref[:, KV_LORA:MLA_WP] = jnp.where(keep, kr, 0.0)
    for h in range(N_HEADS):
        qf_ref[h, :, 0:KV_LORA] = q_lat[:, h * KV_LORA:(h + 1) * KV_LORA]
        shifted = q_rope if h == 0 else pltpu.roll(q_rope, 128 - h * ROPE_DIM, 1)
        qf_ref[h, :, KV_LORA:MLA_WP] = jnp.where(keep, shifted, 0.0)


def _mla_prep(p, lw, pos, tm, tiles_per_seq):
    rows = p.shape[0]
    cos, sin = _rope_tables(pos, 16, 4)
    if pos.shape[0] == 1:
        tab = pl.BlockSpec((1, 128), lambda i: (0, 0))
    else:
        tab = pl.BlockSpec((tm, 128), lambda i: (i % tiles_per_seq, 0))
    full = lambda a, b: pl.BlockSpec((a, b), lambda i: (0, 0))
    return pl.pallas_call(
        _mla_prep_kernel,
        grid=(rows // tm,),
        in_specs=[pl.BlockSpec((tm, 256), lambda i: (i, P_CQ // 256)),
                  pl.BlockSpec((tm, 256), lambda i: (i, P_CKV // 256)),
                  pl.BlockSpec((tm, 128), lambda i: (i, P_KRFL // 128)),
                  tab, tab, full(1, 256), full(1, 256), full(256, 256), full(256, 128), full(256, 1024)],
        out_specs=[pl.BlockSpec((N_HEADS, tm, MLA_WP), lambda i: (0, i, 0)),
                   pl.BlockSpec((tm, MLA_WP), lambda i: (i, 0))],
        out_shape=[jax.ShapeDtypeStruct((N_HEADS, rows, MLA_WP), F32),
                   jax.ShapeDtypeStruct((rows, MLA_WP), F32)],
        compiler_params=_cparams(("parallel",)),
        name="mla_prep",
    )(p, p, p, cos, sin, lw["qn_g"], lw["kvn_g"], lw["w_nope"], lw["w_rope"], lw["wuk_bd"])


def _tri_schedule(n):
    qi = np.concatenate([np.full((i + 1,), i, np.int32) for i in range(n)])
    kj = np.concatenate([np.arange(i + 1, dtype=np.int32) for i in range(n)])
    return jnp.asarray(qi), jnp.asarray(kj)


def _online_softmax(s, m_ref, l_ref):
    m_prev = m_ref[...]
    m_new = jnp.maximum(m_prev, jnp.max(s, axis=-1, keepdims=True))
    alpha = jnp.exp(m_prev - m_new)
    p = jnp.exp(s - m_new)
    l_ref[...] = alpha * l_ref[...] + jnp.sum(p, axis=-1, keepdims=True)
    m_ref[...] = m_new
    return p, alpha


def _mla_flash_kernel(qi_ref, kj_ref, q_ref, kv_ref, wuv_ref, o_ref, m_scr, l_scr, acc_scr, *, tq):
    step = pl.program_id(1)
    i, j = qi_ref[step], kj_ref[step]

    @pl.when(j == 0)
    def _():
        m_scr[...] = jnp.full_like(m_scr, NEG)
        l_scr[...] = jnp.zeros_like(l_scr)
        acc_scr[...] = jnp.zeros_like(acc_scr)

    q = q_ref[...].reshape(N_HEADS * tq, MLA_WP)
    kv = kv_ref[...].astype(BF16)
    s = _bdot_nt(q, kv)
    row = lax.broadcasted_iota(jnp.int32, s.shape, 0) % tq
    col = lax.broadcasted_iota(jnp.int32, s.shape, 1)
    s = jnp.where((j < i) | (col <= row), s, NEG)
    p, alpha = _online_softmax(s, m_scr, l_scr)
    acc_scr[...] = alpha * acc_scr[...] + jnp.dot(p.astype(BF16), kv[:, 0:KV_LORA], preferred_element_type=F32)

    @pl.when(j == i)
    def _():
        o = acc_scr[...] / l_scr[...]
        o_ref[...] = jnp.concatenate(
            [_bdot(o[h * tq:(h + 1) * tq], wuv_ref[h]) for h in range(N_HEADS)], axis=-1)


def _mla_flash(qf, rows, wuv, nseq, seq_len, tq):
    nq = seq_len // tq
    qi, kj = _tri_schedule(nq)
    grid_spec = pltpu.PrefetchScalarGridSpec(
        num_scalar_prefetch=2,
        grid=(nseq, int(qi.shape[0])),
        in_specs=[pl.BlockSpec((N_HEADS, tq, MLA_WP), lambda n, s, qi, kj: (0, n * nq + qi[s], 0)),
                  pl.BlockSpec((tq, MLA_WP), lambda n, s, qi, kj: (n * nq + kj[s], 0)),
                  pl.BlockSpec((N_HEADS, KV_LORA, HEAD_DIM), lambda n, s, qi, kj: (0, 0, 0))],
        out_specs=pl.BlockSpec((tq, 256), lambda n, s, qi, kj: (n * nq + qi[s], 0)),
        scratch_shapes=[pltpu.VMEM((N_HEADS * tq, 1), F32), pltpu.VMEM((N_HEADS * tq, 1), F32),
                        pltpu.VMEM((N_HEADS * tq, KV_LORA), F32)])
    return pl.pallas_call(
        functools.partial(_mla_flash_kernel, tq=tq),
        grid_spec=grid_spec,
        out_shape=jax.ShapeDtypeStruct((nseq * seq_len, 256), F32),
        compiler_params=_cparams(("parallel", "arbitrary")),
        name="mla_prompt_attention",
    )(qi, kj, qf, rows, wuv)


def _mla_decode_kernel(pt_ref, q_ref, new_ref, wuv_ref, *rest, pg):
    pages = rest[:pg]
    o_ref = rest[pg]
    m_scr, l_scr, acc_scr = rest[pg + 1:]
    j = pl.program_id(1)

    @pl.when(j == 0)
    def _():
        m_scr[...] = jnp.full_like(m_scr, NEG)
        l_scr[...] = jnp.zeros_like(l_scr)
        acc_scr[...] = jnp.zeros_like(acc_scr)

    q = q_ref[...].astype(BF16)
    q_lat, q_rope = q[:, 0:KV_LORA], q[:, KV_LORA:MLA_W]
    ks = [pg_ref[...].astype(BF16) for pg_ref in pages]
    s = jnp.concatenate(
        [lax.dot_general(q_lat, k[:, 0:KV_LORA], (((1,), (1,)), ((), ())), preferred_element_type=F32)
         + lax.dot_general(q_rope, k[:, KV_LORA:MLA_W], (((1,), (1,)), ((), ())), preferred_element_type=F32)
         for k in ks], axis=-1)
    p, alpha = _online_softmax(s, m_scr, l_scr)
    p = p.astype(BF16)
    acc = alpha * acc_scr[...]
    for i, k in enumerate(ks):
        acc = acc + jnp.dot(p[:, i * PAGE:(i + 1) * PAGE], k[:, 0:KV_LORA], preferred_element_type=F32)
    acc_scr[...] = acc

    @pl.when(j == pl.num_programs(1) - 1)
    def _():
        new = new_ref[...]
        s_new = jnp.sum(q_ref[...] * new, axis=-1, keepdims=True)
        m_prev = m_scr[...]
        m_new = jnp.maximum(m_prev, s_new)
        a = jnp.exp(m_prev - m_new)
        p_new = jnp.exp(s_new - m_new)
        l = a * l_scr[...] + p_new
        o = (a * acc_scr[...] + p_new * new[:, 0:KV_LORA]) / l
        o_ref[...] = jnp.concatenate([_bdot(o[h:h + 1], wuv_ref[h]) for h in range(N_HEADS)], axis=-1)


def _mla_decode(cache, layer, page_table, q, new_rows, wuv, pg):
    nb, n_pages = page_table.shape
    page_spec = lambda i: pl.BlockSpec((None, None, PAGE, MLA_W),
                                       lambda b, j, pt, i=i: (layer, pt[b, j * pg + i], 0, 0))
    grid_spec = pltpu.PrefetchScalarGridSpec(
        num_scalar_prefetch=1,
        grid=(nb, n_pages // pg),
        in_specs=[pl.BlockSpec((None, 8, MLA_WP), lambda b, j, pt: (b, 0, 0)),
                  pl.BlockSpec((None, 1, MLA_WP), lambda b, j, pt: (b, 0, 0)),
                  pl.BlockSpec((N_HEADS, KV_LORA, HEAD_DIM), lambda b, j, pt: (0, 0, 0))]
                 + [page_spec(i) for i in range(pg)],
        out_specs=pl.BlockSpec((None, 1, 256), lambda b, j, pt: (b, 0, 0)),
        scratch_shapes=[pltpu.VMEM((8, 1), F32), pltpu.VMEM((8, 1), F32), pltpu.VMEM((8, KV_LORA), F32)])
    out = pl.pallas_call(
        functools.partial(_mla_decode_kernel, pg=pg),
        grid_spec=grid_spec,
        out_shape=jax.ShapeDtypeStruct((nb, 1, 256), F32),
        compiler_params=_cparams(("parallel", "arbitrary")),
        name="mla_paged_attention",
    )(page_table, q, new_rows, wuv, *([cache] * pg))
    return out.reshape(nb, 256)


def _log_sigmoid(x):
    return jnp.minimum(x, 0.0) - jnp.log(1.0 + jnp.exp(-jnp.abs(x)))


def _fox_prep_kernel(fq_ref, fk_ref, fv_ref, fl_ref, bf_ref, tri_ref, q_o, k_o, v_o, lf_o, c_o, ct_o, carry,
                     *, cumulative):
    fq = fq_ref[...] * FOX_SCALE
    fk, fv = fk_ref[...], fv_ref[...]
    for h in range(N_HEADS):
        q_o[h] = fq[:, h * HEAD_DIM:(h + 1) * HEAD_DIM]
    for g in range(2):
        k_o[g] = fk[:, g * HEAD_DIM:(g + 1) * HEAD_DIM]
        v_o[g] = fv[:, g * HEAD_DIM:(g + 1) * HEAD_DIM]
    x = pltpu.roll(fl_ref[...] + bf_ref[...], 128 - ROPE_DIM, 1)
    lane = lax.broadcasted_iota(jnp.int32, x.shape, 1)
    lf = jnp.where(lane < N_HEADS, _log_sigmoid(x), 0.0)
    lf_o[...] = lf
    if cumulative:
        @pl.when(pl.program_id(1) == 0)
        def _():
            carry[...] = jnp.zeros_like(carry)
        c = _dot01_left(tri_ref[...], lf) + carry[...]
        carry[...] = c[c.shape[0] - 1:c.shape[0], :]
    else:
        c = lf
    c_o[...] = c
    ct_o[...] = jnp.transpose(c)[0:8, :]


def _fox_prep(p, bf, tm, nseq, seq_len, cumulative):
    rows = p.shape[0]
    tps = seq_len // tm if cumulative else rows // tm
    gn = nseq if cumulative else 1
    tri = (jnp.arange(tm)[:, None] >= jnp.arange(tm)[None, :]).astype(BF16)
    idx = lambda w, off: (lambda n, i: (n * tps + i, off // w))
    out_idx = lambda n, i: (n * tps + i, 0)
    hm = lambda k: pl.BlockSpec((k, tm, HEAD_DIM), lambda n, i: (0, n * tps + i, 0))
    return pl.pallas_call(
        functools.partial(_fox_prep_kernel, cumulative=cumulative),
        grid=(gn, tps),
        in_specs=[pl.BlockSpec((tm, 256), idx(256, P_FQ)), pl.BlockSpec((tm, 128), idx(128, P_FK)),
                  pl.BlockSpec((tm, 128), idx(128, P_FV)), pl.BlockSpec((tm, 128), idx(128, P_KRFL)),
                  pl.BlockSpec((1, 128), lambda n, i: (0, 0)), pl.BlockSpec((tm, tm), lambda n, i: (0, 0))],
        out_specs=[hm(4), hm(2), hm(2), pl.BlockSpec((tm, 128), out_idx), pl.BlockSpec((tm, 128), out_idx),
                   pl.BlockSpec((8, tm), lambda n, i: (0, n * tps + i))],
        out_shape=[jax.ShapeDtypeStruct((4, rows, HEAD_DIM), F32), jax.ShapeDtypeStruct((2, rows, HEAD_DIM), F32),
                   jax.ShapeDtypeStruct((2, rows, HEAD_DIM), F32), jax.ShapeDtypeStruct((rows, 128), F32),
                   jax.ShapeDtypeStruct((rows, 128), F32), jax.ShapeDtypeStruct((8, rows), F32)],
        scratch_shapes=[pltpu.VMEM((1, 128), F32)],
        compiler_params=_cparams(("parallel", "arbitrary")),
        name="fox_prep",
    )(p, p, p, p, bf, tri)


def _fox_flash_kernel(qi_ref, kj_ref, q_ref, k_ref, v_ref, c_ref, ct_ref, o_ref, m_scr, l_scr, acc_scr, *, tq):
    step = pl.program_id(1)
    i, j = qi_ref[step], kj_ref[step]

    @pl.when(j == 0)
    def _():
        m_scr[...] = jnp.full_like(m_scr, NEG)
        l_scr[...] = jnp.zeros_like(l_scr)
        acc_scr[...] = jnp.zeros_like(acc_scr)

    row = lax.broadcasted_iota(jnp.int32, (tq, tq), 0)
    col = lax.broadcasted_iota(jnp.int32, (tq, tq), 1)
    visible = (j < i) | (col <= row)
    c = c_ref[...]
    for h in range(N_HEADS):
        g = h // 2
        s = _bdot_nt(q_ref[h], k_ref[g]) + (c[:, h:h + 1] - ct_ref[h:h + 1, :])
        s = jnp.where(visible, s, NEG)
        p, alpha = _online_softmax(s, m_scr.at[h], l_scr.at[h])
        acc_scr[h] = alpha * acc_scr[h] + _bdot(p, v_ref[g])

    @pl.when(j == i)
    def _():
        o_ref[...] = jnp.concatenate([acc_scr[h] / l_scr[h] for h in range(N_HEADS)], axis=-1)


def _fox_flash(q, k, v, c, ct, nseq, seq_len, tq):
    nq = seq_len // tq
    qi, kj = _tri_schedule(nq)
    qmap = lambda n, s, qi, kj: (0, n * nq + qi[s], 0)
    kmap = lambda n, s, qi, kj: (0, n * nq + kj[s], 0)
    grid_spec = pltpu.PrefetchScalarGridSpec(
        num_scalar_prefetch=2,
        grid=(nseq, int(qi.shape[0])),
        in_specs=[pl.BlockSpec((4, tq, HEAD_DIM), qmap), pl.BlockSpec((2, tq, HEAD_DIM), kmap),
                  pl.BlockSpec((2, tq, HEAD_DIM), kmap),
                  pl.BlockSpec((tq, 128), lambda n, s, qi, kj: (n * nq + qi[s], 0)),
                  pl.BlockSpec((8, tq), lambda n, s, qi, kj: (0, n * nq + kj[s]))],
        out_specs=pl.BlockSpec((tq, 256), lambda n, s, qi, kj: (n * nq + qi[s], 0)),
        scratch_shapes=[pltpu.VMEM((4, tq, 1), F32), pltpu.VMEM((4, tq, 1), F32),
                        pltpu.VMEM((4, tq, HEAD_DIM), F32)])
    return pl.pallas_call(
        functools.partial(_fox_flash_kernel, tq=tq),
        grid_spec=grid_spec,
        out_shape=jax.ShapeDtypeStruct((nseq * seq_len, 256), F32),
        compiler_params=_cparams(("parallel", "arbitrary")),
        name="fox_prompt_attention",
    )(qi, kj, q, k, v, c, ct)


def _fox_decode_kernel(pt_ref, q_ref, kn_ref, vn_ref, cn_ref, slt_ref, *rest, pg):
    kps, vps, lps = rest[:pg], rest[pg:2 * pg], rest[2 * pg:3 * pg]
    o_ref = rest[3 * pg]
    m_scr, l_scr, acc_scr, run_scr, lf_scr = rest[3 * pg + 1:]
    j = pl.program_id(1)

    @pl.when(j == 0)
    def _():
        m_scr[...] = jnp.full_like(m_scr, NEG)
        l_scr[...] = jnp.zeros_like(l_scr)
        acc_scr[...] = jnp.zeros_like(acc_scr)
        run_scr[...] = jnp.zeros_like(run_scr)
        lf_scr[...] = jnp.zeros_like(lf_scr)

    for i in range(pg):
        lf_scr[i * 8:i * 8 + N_HEADS, :] = lps[i][...]
    lf = lf_scr[...]
    within = _dot01(lf, slt_ref[...])
    totals = jnp.sum(lf, axis=-1, keepdims=True)
    q = q_ref[...].astype(BF16)
    run = run_scr[...] + cn_ref[...]
    tiles = []
    vs = []
    for i in range(pg):
        k = kps[i][...].astype(BF16)
        vs.append(vps[i][...].astype(BF16))
        s = lax.dot_general(q, k, (((1,), (1,)), ((), ())), preferred_element_type=F32)
        tiles.append(s + within[i * 8:(i + 1) * 8, :] + run)
        run = run + totals[i * 8:(i + 1) * 8, :]
    run_scr[...] = run - cn_ref[...]
    s = jnp.concatenate(tiles, axis=-1)
    p, alpha = _online_softmax(s, m_scr, l_scr)
    p = p.astype(BF16)
    acc = alpha * acc_scr[...]
    for i in range(pg):
        acc = acc + jnp.dot(p[:, i * PAGE:(i + 1) * PAGE], vs[i], preferred_element_type=F32)
    acc_scr[...] = acc

    @pl.when(j == pl.num_programs(1) - 1)
    def _():
        s_new = jnp.sum(q_ref[...] * kn_ref[...], axis=-1, keepdims=True)
        m_prev = m_scr[...]
        m_new = jnp.maximum(m_prev, s_new)
        a = jnp.exp(m_prev - m_new)
        p_new = jnp.exp(s_new - m_new)
        l = a * l_scr[...] + p_new
        o = (a * acc_scr[...] + p_new * vn_ref[...]) / l
        o_ref[...] = jnp.concatenate(
            [o[h:h + 1, (h // 2) * HEAD_DIM:(h // 2 + 1) * HEAD_DIM] for h in range(N_HEADS)], axis=-1)


def _fox_decode(cache_k, cache_v, cache_lft, layer, page_table, q8, k_new, v_new, cn8, pg):
    nb, n_pages = page_table.shape
    slt = (jnp.arange(PAGE)[:, None] > jnp.arange(PAGE)[None, :]).astype(BF16)
    newest_first = lambda b, j, pt, i: pt[b, n_pages - 1 - (j * pg + i)]
    kv_spec = lambda i: pl.BlockSpec((None, None, PAGE, 128),
                                     lambda b, j, pt, i=i: (layer, newest_first(b, j, pt, i), 0, 0))
    lf_spec = lambda i: pl.BlockSpec((None, None, N_HEADS, PAGE),
                                     lambda b, j, pt, i=i: (layer, newest_first(b, j, pt, i), 0, 0))
    per_seq = lambda r, w: pl.BlockSpec((None, r, w), lambda b, j, pt: (b, 0, 0))
    grid_spec = pltpu.PrefetchScalarGridSpec(
        num_scalar_prefetch=1,
        grid=(nb, n_pages // pg),
        in_specs=[per_seq(8, 128), per_seq(1, 128), per_seq(1, 128), per_seq(8, 1),
                  pl.BlockSpec((PAGE, PAGE), lambda b, j, pt: (0, 0))]
                 + [kv_spec(i) for i in range(pg)] + [kv_spec(i) for i in range(pg)]
                 + [lf_spec(i) for i in range(pg)],
        out_specs=pl.BlockSpec((None, 1, 256), lambda b, j, pt: (b, 0, 0)),
        scratch_shapes=[pltpu.VMEM((8, 1), F32), pltpu.VMEM((8, 1), F32), pltpu.VMEM((8, 128), F32),
                        pltpu.VMEM((8, 1), F32), pltpu.VMEM((pg * 8, PAGE), F32)])
    out = pl.pallas_call(
        functools.partial(_fox_decode_kernel, pg=pg),
        grid_spec=grid_spec,
        out_shape=jax.ShapeDtypeStruct((nb, 1, 256), F32),
        compiler_params=_cparams(("parallel", "arbitrary")),
        name="fox_paged_attention",
    )(page_table, q8, k_new, v_new, cn8, slt, *([cache_k] * pg), *([cache_v] * pg), *([cache_lft] * pg))
    return out.reshape(nb, 256)


def _merge_kernel(x_ref, pg_ref, ya_ref, yb_ref, yc_ref, yd_ref, wb_ref, wo_ref, o_ref):
    pg = pg_ref[...]
    merged = None
    for bi, y_ref in enumerate((ya_ref, yb_ref, yc_ref, yd_ref)):
        gate = _sigmoid(pg[:, bi * D_MODEL:(bi + 1) * D_MODEL])
        term = gate * jnp.dot(y_ref[...].astype(BF16), wb_ref[bi], preferred_element_type=F32)
        merged = term if merged is None else merged + term
    o_ref[...] = x_ref[...] + jnp.dot(merged.astype(BF16), wo_ref[...], preferred_element_type=F32)


def _merge(x, p, ya, yb, yc, yd, wb, wo, tm):
    rows = x.shape[0]
    row = lambda w: pl.BlockSpec((tm, w), lambda i: (i, 0))
    return pl.pallas_call(
        _merge_kernel,
        grid=(rows // tm,),
        in_specs=[row(D_MODEL), pl.BlockSpec((tm, 4 * D_MODEL), lambda i: (i, P_GATE)),
                  row(256), row(256), row(256), row(256),
                  pl.BlockSpec((4, BRANCH_W, D_MODEL), lambda i: (0, 0, 0)),
                  pl.BlockSpec((D_MODEL, D_MODEL), lambda i: (0, 0))],
        out_specs=row(D_MODEL),
        out_shape=jax.ShapeDtypeStruct((rows, D_MODEL), F32),
        compiler_params=_cparams(("parallel",)),
        name="gated_merge",
    )(x, p, ya, yb, yc, yd, wb, wo)


def _mlp_kernel(x_ref, g_ref, wu_ref, wd_ref, fg_ref, *o_refs, final):
    x = x_ref[...]
    h = _rms(x, g_ref[...]).astype(BF16)
    u = jnp.maximum(jnp.dot(h, wu_ref[...], preferred_element_type=F32), 0.0)
    y = x + jnp.dot((u * u).astype(BF16), wd_ref[...], preferred_element_type=F32)
    o_refs[0][...] = y
    if final:
        o_refs[1][...] = _rms(y, fg_ref[...])


def _mlp(x, g, wu, wd, fg, tm, final):
    rows = x.shape[0]
    row = pl.BlockSpec((tm, D_MODEL), lambda i: (i, 0))
    vec = pl.BlockSpec((1, D_MODEL), lambda i: (0, 0))
    const = lambda a, b: pl.BlockSpec((a, b), lambda i: (0, 0), pipeline_mode=pl.Buffered(1))
    n_out = 2 if final else 1
    return pl.pallas_call(
        functools.partial(_mlp_kernel, final=final),
        grid=(rows // tm,),
        in_specs=[row, vec, const(D_MODEL, D_FF), const(D_FF, D_MODEL), vec],
        out_specs=[row] * n_out,
        out_shape=[jax.ShapeDtypeStruct((rows, D_MODEL), F32)] * n_out,
        compiler_params=_cparams(("parallel",)),
        name="mlp",
    )(x, g, wu, wd, fg)


def _layer(x, lw, consts, *, nseq, seq_len, tm, decode):
    rows = x.shape[0]
    p = _norm_matmul(x, lw["norm1_g"], lw["w_in"], tm)
    prompt = decode is None

    if prompt:
        r, w, kh, v, kk, b, g = _rwkv_prep(p, None, lw, consts["bd"], tm, True, seq_len)
        y, rwkv_new = _rwkv_scan(r, w, kh, v, kk, b, nseq, seq_len, 128)
    else:
        r, w, kh, v, kk, b, g = _rwkv_prep(p, decode["shift"], lw, consts["bd"], tm, False, seq_len)
        y, rwkv_new = _rwkv_single(r, w, kh, v, kk, b, decode["rwkv"])
    ya = _rwkv_post(y, r, kh, v, g, lw, tm)
    shift_new = p[:, P_RWKV:P_RWKV + RWKV_IN].reshape(nseq, seq_len, RWKV_IN)[:, -1]

    if prompt:
        yb, ret_new = _ret_prompt(p, nseq, seq_len)
    else:
        yb, ret_new = _ret_single(p, decode["ret"], decode["pos"])

    pos = jnp.arange(seq_len) if prompt else jnp.full((1,), decode["pos"])
    qf, mla_rows = _mla_prep(p, lw, pos, tm, max(seq_len // tm, 1))
    if prompt:
        yc = _mla_flash(qf, mla_rows, lw["wuv"], nseq, seq_len, min(256, seq_len))
    else:
        q8 = jnp.pad(jnp.transpose(qf, (1, 0, 2)), ((0, 0), (0, 8 - N_HEADS), (0, 0)))
        yc = _mla_decode(decode["cache_mla"], decode["layer"], decode["page_table"], q8,
                         mla_rows.reshape(rows, 1, MLA_WP), lw["wuv"], decode["pg_mla"])

    fq, fk, fv, lf, c, ct = _fox_prep(p, lw["bf"], tm, nseq, seq_len, prompt)
    if prompt:
        yd = _fox_flash(fq, fk, fv, c, ct, nseq, seq_len, min(256, seq_len))
    else:
        q8 = jnp.zeros((rows, 8, 128), F32)
        for h in range(N_HEADS):
            gq = (h // 2) * HEAD_DIM
            q8 = q8.at[:, h, gq:gq + HEAD_DIM].set(fq[h])
        cn8 = jnp.pad(lf[:, 0:N_HEADS], ((0, 0), (0, 8 - N_HEADS))).reshape(rows, 8, 1)
        k_new = p[:, P_FK:P_FK + 128].reshape(rows, 1, 128)
        v_new = p[:, P_FV:P_FV + 128].reshape(rows, 1, 128)
        yd = _fox_decode(decode["cache_k"], decode["cache_v"], decode["cache_lft"], decode["layer"],
                         decode["page_table"], q8, k_new, v_new, cn8, decode["pg_fox"])

    x1 = _merge(x, p, ya, yb, yc, yd, lw["w_branch"], lw["w_out"], tm)
    new = (mla_rows[:, 0:MLA_W], p[:, P_FK:P_FK + 128], p[:, P_FV:P_FV + 128], lf[:, 0:N_HEADS],
           rwkv_new, shift_new, ret_new)
    return x1, new


def _layer_weights(l, norm1_g, norm2_g, w_in, rwkv_mu, rwkv_w0, rwkv_w2, rwkv_a0, rwkv_a2, rwkv_g2, rwkv_kk,
                   rwkv_ka, rwkv_rk, rwkv_ln_g, rwkv_ln_b, mla_qn_g, mla_kvn_g, mla_wuq, mla_wuk, mla_wuv,
                   fox_bf, w_branch, w_out, w_up, w_down):
    wi = w_in[l]
    a, b = wi[:, 0:1024], wi[:, 1024:2048]
    cq, ckv, kr = wi[:, 2048:2240], wi[:, 2240:2496], wi[:, 2496:2528]
    fq, fk, fv, fl = wi[:, 2528:2784], wi[:, 2784:2912], wi[:, 2912:3040], wi[:, 3040:3044]
    gate = wi[:, 3044:7140]
    z = lambda n: jnp.zeros((D_MODEL, n), F32)
    w_all = jnp.concatenate([gate, a, b, fq, fk, fv, cq, z(256 - Q_LORA), ckv, kr, fl, z(128 - ROPE_DIM - 4)],
                            axis=1).astype(BF16)
    row = lambda v: v.reshape(1, -1)
    wuq = jnp.pad(mla_wuq[l], ((0, 256 - Q_LORA), (0, 0), (0, 0)))
    wuk_bd = jnp.zeros((N_HEADS * NOPE_DIM, N_HEADS * KV_LORA), F32)
    for h in range(N_HEADS):
        wuk_bd = wuk_bd.at[h * NOPE_DIM:(h + 1) * NOPE_DIM, h * KV_LORA:(h + 1) * KV_LORA].set(mla_wuk[l][:, h, :].T)
    return dict(
        norm1_g=row(norm1_g[l]), norm2_g=row(norm2_g[l]), w_in=w_all,
        mu=row(rwkv_mu[l]), w0=row(rwkv_w0[l]), w2=rwkv_w2[l].astype(BF16), a0=row(rwkv_a0[l]),
        a2=rwkv_a2[l].astype(BF16), g2=rwkv_g2[l].astype(BF16), kkp=row(rwkv_kk[l]), ka=row(rwkv_ka[l]),
        rk=rwkv_rk[l].reshape(N_HEADS, 1, HEAD_DIM), ln_g=row(rwkv_ln_g[l]), ln_b=row(rwkv_ln_b[l]),
        qn_g=row(jnp.pad(mla_qn_g[l], (0, 256 - Q_LORA))), kvn_g=row(mla_kvn_g[l]),
        w_nope=wuq[:, :, 0:NOPE_DIM].reshape(256, N_HEADS * NOPE_DIM).astype(BF16),
        w_rope=wuq[:, :, NOPE_DIM:].reshape(256, N_HEADS * ROPE_DIM).astype(BF16),
        wuk_bd=wuk_bd.astype(BF16),
        wuv=jnp.transpose(mla_wuv[l], (1, 0, 2)).astype(BF16),
        bf=jnp.pad(fox_bf[l], (ROPE_DIM, 128 - ROPE_DIM - 4)).reshape(1, 128),
        w_branch=w_branch[l].astype(BF16), w_out=w_out[l].astype(BF16),
        w_up=w_up[l].astype(BF16), w_down=w_down[l].astype(BF16))


def kernel(x_prompt, x_sample, cache_mla, cache_fox_k, cache_fox_v, cache_fox_logf, state_rwkv, state_rwkv_shift, state_ret, page_table, norm1_g, norm2_g, final_g, w_in, rwkv_mu, rwkv_w0, rwkv_w2, rwkv_a0, rwkv_a2, rwkv_g2, rwkv_kk, rwkv_ka, rwkv_rk, rwkv_ln_g, rwkv_ln_b, mla_qn_g, mla_kvn_g, mla_wuq, mla_wuk, mla_wuv, fox_bf, w_branch, w_out, w_up, w_down):
    nseq, seq_len = x_prompt.shape[:2]
    nb, dec_len = x_sample.shape[:2]
    assert dec_len == 1
    depth = w_in.shape[0]
    n_pages = page_table.shape[1]
    t_past = n_pages * PAGE
    n_pool = cache_mla.shape[1]
    pg_mla = min(32, n_pages)
    pg_fox = min(16, n_pages)
    tm_p = min(512, seq_len)
    tm_s = nb

    cache_k = cache_fox_k.reshape(depth, n_pool, PAGE, 128)
    cache_v = cache_fox_v.reshape(depth, n_pool, PAGE, 128)
    cache_lft = jnp.transpose(cache_fox_logf, (0, 1, 3, 2))
    consts = dict(bd=(jnp.arange(256)[:, None] // HEAD_DIM == jnp.arange(256)[None, :] // HEAD_DIM).astype(BF16))
    fg = final_g.reshape(1, D_MODEL)

    xp = x_prompt.reshape(nseq * seq_len, D_MODEL)
    xs = x_sample.reshape(nb, D_MODEL)
    new_p, new_s = [], []
    yp = ys = None
    for l in range(depth):
        lw = _layer_weights(l, norm1_g, norm2_g, w_in, rwkv_mu, rwkv_w0, rwkv_w2, rwkv_a0, rwkv_a2, rwkv_g2,
                            rwkv_kk, rwkv_ka, rwkv_rk, rwkv_ln_g, rwkv_ln_b, mla_qn_g, mla_kvn_g, mla_wuq,
                            mla_wuk, mla_wuv, fox_bf, w_branch, w_out, w_up, w_down)
        final = l == depth - 1
        x1, st = _layer(xp, lw, consts, nseq=nseq, seq_len=seq_len, tm=tm_p, decode=None)
        outs = _mlp(x1, lw["norm2_g"], lw["w_up"], lw["w_down"], fg, tm_p, final)
        xp = outs[0]
        if final:
            yp = outs[1]
        new_p.append(st)
        decode = dict(shift=state_rwkv_shift[l], rwkv=state_rwkv[l], ret=state_ret[l], pos=t_past, layer=l,
                      page_table=page_table, cache_mla=cache_mla, cache_k=cache_k, cache_v=cache_v,
                      cache_lft=cache_lft, pg_mla=pg_mla, pg_fox=pg_fox)
        x1, st = _layer(xs, lw, consts, nseq=nb, seq_len=1, tm=tm_s, decode=decode)
        outs = _mlp(x1, lw["norm2_g"], lw["w_up"], lw["w_down"], fg, tm_s, final)
        xs = outs[0]
        if final:
            ys = outs[1]
        new_s.append(st)

    def stack(new, i, shape):
        return jnp.stack([st[i] for st in new]).reshape((depth,) + shape)

    res = [yp.reshape(nseq, seq_len, D_MODEL), ys.reshape(nb, 1, D_MODEL)]
    shapes_p = [(nseq, seq_len, MLA_W), (nseq, seq_len, 2, HEAD_DIM), (nseq, seq_len, 2, HEAD_DIM),
                (nseq, seq_len, N_HEADS), (nseq, N_HEADS, HEAD_DIM, HEAD_DIM), (nseq, RWKV_IN),
                (nseq, N_HEADS, HEAD_DIM, HEAD_DIM)]
    shapes_s = [(nb, 1, MLA_W), (nb, 1, 2, HEAD_DIM), (nb, 1, 2, HEAD_DIM), (nb, 1, N_HEADS),
                (nb, N_HEADS, HEAD_DIM, HEAD_DIM), (nb, RWKV_IN), (nb, N_HEADS, HEAD_DIM, HEAD_DIM)]
    for i in range(7):
        res.append(stack(new_p, i, shapes_p[i]))
        res.append(stack(new_s, i, shapes_s[i]))
    return tuple(res)
```

```python
import functools

import jax
import jax.numpy as jnp
import numpy as np
from jax import lax
from jax.experimental import pallas as pl
from jax.experimental.pallas import tpu as pltpu

F32 = jnp.float32
BF16 = jnp.bfloat16

D_MODEL = 1024
HEAD_DIM = 64
N_HEADS = 4
BRANCH_W = 256
PAGE = 128
DECAY_LORA = 64
ICLR_LORA = 64
GATE_LORA = 128
RWKV_IN = 1024
DECAY_SCALE = 0.6065306597126334
GN_EPS = 64e-5
RET_CHUNK = 128
Q_LORA = 192
KV_LORA = 256
NOPE_DIM = 64
ROPE_DIM = 32
MLA_W = KV_LORA + ROPE_DIM
MLA_WP = 384
MLA_SCALE = (NOPE_DIM + ROPE_DIM) ** -0.5
FOX_SCALE = HEAD_DIM ** -0.5
D_FF = 4096
ROPE_BASE = 10000.0
NORM_EPS = 1e-6
NEG = -1e30

P_GATE = 0
P_RWKV = 4096
P_RET = 5120
P_FQ = 6144
P_FK = 6400
P_FV = 6528
P_CQ = 6656
P_CKV = 6912
P_KRFL = 7168
P_W = 7296
P_TN = 2432

VMEM_LIMIT = 56 * 1024 * 1024
ATTN_TILE = 512


def _cparams(sem):
    return pltpu.CompilerParams(dimension_semantics=sem, vmem_limit_bytes=VMEM_LIMIT)


def _bdot(a, b):
    return jnp.dot(a.astype(BF16), b.astype(BF16), preferred_element_type=F32)


def _bdot_nt(a, b):
    return lax.dot_general(a.astype(BF16), b.astype(BF16), (((1,), (1,)), ((), ())),
                           preferred_element_type=F32)


def _bdot_tn(a, b):
    return lax.dot_general(a.astype(BF16), b.astype(BF16), (((0,), (0,)), ((), ())),
                           preferred_element_type=F32)


def _split3(a):
    a1 = a.astype(BF16)
    r1 = a - a1.astype(F32)
    a2 = r1.astype(BF16)
    a3 = (r1 - a2.astype(F32)).astype(BF16)
    return a1, a2, a3


def _dot01(a, ones01):
    a1, a2, a3 = _split3(a)
    d = lambda p: jnp.dot(p, ones01, preferred_element_type=F32)
    return d(a1) + (d(a2) + d(a3))


def _dot01_left(ones01, a):
    a1, a2, a3 = _split3(a)
    d = lambda p: jnp.dot(ones01, p, preferred_element_type=F32)
    return d(a1) + (d(a2) + d(a3))


def _sigmoid(x):
    return 1.0 / (1.0 + jnp.exp(-x))


def _rms(x, g, n=None):
    n = x.shape[-1] if n is None else n
    ms = jnp.sum(x * x, axis=-1, keepdims=True) * (1.0 / n)
    return x * lax.rsqrt(ms + NORM_EPS) * g


def _rotate(x, cos, sin_signed, half, width):
    lane = lax.broadcasted_iota(jnp.int32, x.shape, x.ndim - 1)
    first = (lane % (2 * half)) < half
    partner = jnp.where(first, pltpu.roll(x, width - half, x.ndim - 1), pltpu.roll(x, half, x.ndim - 1))
    return x * cos + partner * sin_signed


def _norm_matmul_kernel(x_ref, g_ref, w_ref, o_ref):
    h = _rms(x_ref[...], g_ref[...])
    o_ref[...] = jnp.dot(h.astype(BF16), w_ref[...], preferred_element_type=F32)


def _norm_matmul(x, g, w, tm):
    rows = x.shape[0]
    return pl.pallas_call(
        _norm_matmul_kernel,
        grid=(P_W // P_TN, rows // tm),
        in_specs=[pl.BlockSpec((tm, D_MODEL), lambda j, i: (i, 0)),
                  pl.BlockSpec((1, D_MODEL), lambda j, i: (0, 0)),
                  pl.BlockSpec((D_MODEL, P_TN), lambda j, i: (0, j))],
        out_specs=pl.BlockSpec((tm, P_TN), lambda j, i: (i, j)),
        out_shape=jax.ShapeDtypeStruct((rows, P_W), F32),
        compiler_params=_cparams(("parallel", "parallel")),
        name="norm_in_proj",
    )(x, g, w)


def _rwkv_prep_kernel(pa_ref, prev_ref, mu_ref, w0_ref, w2_ref, a0_ref, a2_ref, g2_ref, kkp_ref,
                      ka_ref, bd_ref, r_o, w_o, kh_o, v_o, kk_o, b_o, g_o, *, shifted, tiles_per_seq):
    pa = pa_ref[...]
    if shifted:
        i = pl.program_id(0)
        rolled = pltpu.roll(pa, 1, 0)
        halo = prev_ref[7:8, :]
        halo = jnp.where(i % tiles_per_seq == 0, jnp.zeros_like(halo), halo)
        row = lax.broadcasted_iota(jnp.int32, pa.shape, 0)
        prev = jnp.where(row == 0, halo, rolled)
    else:
        prev = prev_ref[...]
    xm = pa + (prev - pa) * mu_ref[...]
    r = xm[:, 0:256]
    k = xm[:, 256:512]
    v = xm[:, 512:768]
    wl = xm[:, 768:832]
    al = xm[:, 832:896]
    gl = xm[:, 896:1024]
    logw = -DECAY_SCALE * _sigmoid(w0_ref[...] + _bdot(jnp.tanh(wl), w2_ref[...]))
    a = _sigmoid(a0_ref[...] + _bdot(al, a2_ref[...]))
    g = _bdot(_sigmoid(gl), g2_ref[...])
    kk = k * kkp_ref[...]
    ss = _dot01(kk * kk, bd_ref[...])
    kk = kk / jnp.maximum(jnp.sqrt(ss), 1e-12)
    kh = k * (1.0 + (a - 1.0) * ka_ref[...])
    b = kk * a
    for h in range(N_HEADS):
        sl = slice(h * HEAD_DIM, (h + 1) * HEAD_DIM)
        r_o[h] = r[:, sl]
        w_o[h] = logw[:, sl]
        kh_o[h] = kh[:, sl]
        v_o[h] = v[:, sl]
        kk_o[h] = kk[:, sl]
        b_o[h] = b[:, sl]
    g_o[...] = g


def _rwkv_prep(p, prev, lw, bd, tm, shifted, seq_len):
    rows = p.shape[0]
    tiles_per_seq = max(seq_len // tm, 1)
    col = P_RWKV // RWKV_IN
    if shifted:
        prev_arr = p
        prev_spec = pl.BlockSpec((8, RWKV_IN), lambda i: (jnp.maximum(i * (tm // 8) - 1, 0), col))
    else:
        prev_arr = prev
        prev_spec = pl.BlockSpec((tm, RWKV_IN), lambda i: (i, 0))
    vec = lambda n: pl.BlockSpec((1, n), lambda i: (0, 0))
    mat = lambda a, b: pl.BlockSpec((a, b), lambda i: (0, 0))
    hm = jax.ShapeDtypeStruct((N_HEADS, rows, HEAD_DIM), F32)
    hm_spec = pl.BlockSpec((N_HEADS, tm, HEAD_DIM), lambda i: (0, i, 0))
    return pl.pallas_call(
        functools.partial(_rwkv_prep_kernel, shifted=shifted, tiles_per_seq=tiles_per_seq),
        grid=(rows // tm,),
        in_specs=[pl.BlockSpec((tm, RWKV_IN), lambda i: (i, col)), prev_spec,
                  vec(RWKV_IN), vec(256), mat(DECAY_LORA, 256), vec(256), mat(ICLR_LORA, 256),
                  mat(GATE_LORA, 256), vec(256), vec(256), mat(256, 256)],
        out_specs=[hm_spec] * 6 + [pl.BlockSpec((tm, 256), lambda i: (i, 0))],
        out_shape=[hm] * 6 + [jax.ShapeDtypeStruct((rows, 256), F32)],
        compiler_params=_cparams(("parallel",)),
        name="rwkv_prep",
    )(p, prev_arr, lw["mu"], lw["w0"], lw["w2"], lw["a0"], lw["a2"], lw["g2"], lw["kkp"], lw["ka"], bd)


def _rwkv_step(s, r, w, kh, v, kk, b, eye):
    sa = jnp.sum(s * kk, axis=-1, keepdims=True)
    v_col = jnp.sum(jnp.where(eye, v, 0.0), axis=-1, keepdims=True)
    s = s * w - sa * b + v_col * kh
    y_col = jnp.sum(s * r, axis=-1, keepdims=True)
    y_row = jnp.sum(jnp.where(eye, y_col, 0.0), axis=0, keepdims=True)
    return s, y_row


def _eye64():
    return (lax.broadcasted_iota(jnp.int32, (HEAD_DIM, HEAD_DIM), 0)
            == lax.broadcasted_iota(jnp.int32, (HEAD_DIM, HEAD_DIM), 1))


def _dot2(a, b, dims=(((1,), (0,)), ((), ()))):
    ah = a.astype(BF16)
    al = (a - ah.astype(F32)).astype(BF16)
    bh = b.astype(BF16)
    bl = (b - bh.astype(F32)).astype(BF16)
    d = lambda x, y: lax.dot_general(x, y, dims, preferred_element_type=F32)
    return d(ah, bh) + (d(ah, bl) + d(al, bh))


_NT = (((1,), (1,)), ((), ()))
_TN = (((0,), (0,)), ((), ()))
RWKV_CHUNK = 64


def _rwkv_chunk_kernel(r_ref, lw_ref, kh_ref, v_ref, kk_ref, b_ref, tri_ref, y_ref, sfin_ref, s_scr):
    c = pl.program_id(1)

    @pl.when(c == 0)
    def _():
        s_scr[...] = jnp.zeros_like(s_scr)

    C = RWKV_CHUNK
    row = lax.broadcasted_iota(jnp.int32, (C, 2 * C), 0)
    col = lax.broadcasted_iota(jnp.int32, (C, 2 * C), 1)
    left = col < C
    colr = jnp.where(left, col, col - C)
    strict = row > colr
    incl = row >= colr
    eye = (row == col)[:, 0:C]
    tri = tri_ref[...]
    heads = range(N_HEADS)
    lw, r, kh, v, kk, b = ([ref[h] for h in heads] for ref in (lw_ref, r_ref, kh_ref, v_ref, kk_ref, b_ref))
    lp = [_dot01_left(tri, lw[h]) for h in heads]
    lp_end = [lp[h][C - 1:C, :] for h in heads]
    kap = [kk[h] * jnp.exp(lp[h] - lw[h]) for h in heads]
    rt = [r[h] * jnp.exp(lp[h]) for h in heads]
    e_in = [jnp.exp(-lp[h]) for h in heads]
    bk = [jnp.concatenate([b[h] * e_in[h], kh[h] * e_in[h]], axis=0) for h in heads]
    rem = [jnp.exp(lp_end[h] - lp[h]) for h in heads]
    g1 = [_dot2(kap[h], bk[h], _NT) for h in heads]
    g2 = [_bdot_nt(rt[h], bk[h]) for h in heads]
    l = [jnp.where(strict[:, 0:C], g1[h][:, 0:C], 0.0) for h in heads]
    x = [jnp.where(eye, 1.0, 0.0) - l[h] for h in heads]
    p = [_dot2(l[h], l[h]) for h in heads]
    for i in range(5):
        x = [x[h] + _dot2(x[h], p[h]) for h in heads]
        if i < 4:
            p = [_dot2(p[h], p[h]) for h in heads]
    s0 = [s_scr[h] for h in heads]
    not_left_strict = jnp.logical_and(jnp.logical_not(left), strict)
    rhs = [_dot2(kap[h], s0[h], _NT)
           + _dot2(jnp.where(not_left_strict, g1[h], 0.0), jnp.concatenate([v[h], v[h]], axis=0)) for h in heads]
    u = [_dot2(x[h], rhs[h]) for h in heads]
    for h in heads:
        g2m = jnp.where(incl, jnp.where(left, -g2[h], g2[h]), 0.0)
        y_ref[h] = _bdot_nt(rt[h], s0[h]) + _bdot(g2m, jnp.concatenate([u[h], v[h]], axis=0))
    for h in heads:
        s_scr[h] = (s0[h] * jnp.exp(lp_end[h]) + _dot2(v[h], kh[h] * rem[h], _TN)
                    - _dot2(u[h], b[h] * rem[h], _TN))

    @pl.when(c == pl.num_programs(1) - 1)
    def _():
        sfin_ref[...] = s_scr[...]


def _rwkv_scan(r, lw, kh, v, kk, b, nseq, seq_len):
    C = RWKV_CHUNK
    view = lambda a: a.reshape(N_HEADS, nseq, seq_len, HEAD_DIM)
    spec = pl.BlockSpec((N_HEADS, None, C, HEAD_DIM), lambda n, c: (0, n, c, 0))
    tri = (jnp.arange(C)[:, None] >= jnp.arange(C)[None, :]).astype(BF16)
    y, sfin = pl.pallas_call(
        _rwkv_chunk_kernel,
        grid=(nseq, seq_len // C),
        in_specs=[spec] * 6 + [pl.BlockSpec((C, C), lambda n, c: (0, 0))],
        out_specs=[spec, pl.BlockSpec((None, N_HEADS, HEAD_DIM, HEAD_DIM), lambda n, c: (n, 0, 0, 0))],
        out_shape=[jax.ShapeDtypeStruct((N_HEADS, nseq, seq_len, HEAD_DIM), F32),
                   jax.ShapeDtypeStruct((nseq, N_HEADS, HEAD_DIM, HEAD_DIM), F32)],
        scratch_shapes=[pltpu.VMEM((N_HEADS, HEAD_DIM, HEAD_DIM), F32)],
        compiler_params=_cparams(("parallel", "arbitrary")),
        name="rwkv_chunks",
    )(view(r), view(lw), view(kh), view(v), view(kk), view(b), tri)
    return y.reshape(N_HEADS, nseq * seq_len, HEAD_DIM), sfin


def _rwkv_single_kernel(r_ref, w_ref, kh_ref, v_ref, kk_ref, b_ref, s_ref, y_ref, so_ref, *, bs):
    eye = _eye64()
    for n in range(bs):
        for h in range(N_HEADS):
            row = lambda ref: ref[h, n:n + 1, :]
            s, y_row = _rwkv_step(s_ref[n, h], row(r_ref), jnp.exp(row(w_ref)), row(kh_ref), row(v_ref),
                                  row(kk_ref), row(b_ref), eye)
            so_ref[n, h] = s
            y_ref[h, n:n + 1, :] = y_row


def _rwkv_single(r, w, kh, v, kk, b, s0, bs=8):
    nb = s0.shape[0]
    spec = pl.BlockSpec((N_HEADS, bs, HEAD_DIM), lambda i: (0, i, 0))
    s_spec = pl.BlockSpec((bs, N_HEADS, HEAD_DIM, HEAD_DIM), lambda i: (i, 0, 0, 0))
    return pl.pallas_call(
        functools.partial(_rwkv_single_kernel, bs=bs),
        grid=(nb // bs,),
        in_specs=[spec] * 6 + [s_spec],
        out_specs=[spec, s_spec],
        out_shape=[jax.ShapeDtypeStruct((N_HEADS, nb, HEAD_DIM), F32),
                   jax.ShapeDtypeStruct(s0.shape, F32)],
        compiler_params=_cparams(("parallel",)),
        name="rwkv_single",
    )(r, w, kh, v, kk, b, s0)


def _rwkv_post_kernel(y_ref, r_ref, kh_ref, v_ref, g_ref, rk_ref, lng_ref, lnb_ref, o_ref):
    outs = []
    for h in range(N_HEADS):
        bonus = jnp.sum(r_ref[h] * kh_ref[h] * rk_ref[h], axis=-1, keepdims=True)
        y = y_ref[h] + bonus * v_ref[h]
        mu = jnp.mean(y, axis=-1, keepdims=True)
        d = y - mu
        var = jnp.mean(d * d, axis=-1, keepdims=True)
        outs.append(d * lax.rsqrt(var + GN_EPS))
    y = jnp.concatenate(outs, axis=-1) * lng_ref[...] + lnb_ref[...]
    o_ref[...] = y * g_ref[...]


def _rwkv_post(y, r, kh, v, g, lw, tm):
    rows = g.shape[0]
    hm_spec = pl.BlockSpec((N_HEADS, tm, HEAD_DIM), lambda i: (0, i, 0))
    vec = pl.BlockSpec((1, 256), lambda i: (0, 0))
    return pl.pallas_call(
        _rwkv_post_kernel,
        grid=(rows // tm,),
        in_specs=[hm_spec] * 4 + [pl.BlockSpec((tm, 256), lambda i: (i, 0)),
                                  pl.BlockSpec((N_HEADS, 1, HEAD_DIM), lambda i: (0, 0, 0)), vec, vec],
        out_specs=pl.BlockSpec((tm, 256), lambda i: (i, 0)),
        out_shape=jax.ShapeDtypeStruct((rows, 256), F32),
        compiler_params=_cparams(("parallel",)),
        name="rwkv_post",
    )(y, r, kh, v, g, lw["rk"], lw["ln_g"], lw["ln_b"])


def _ret_log_decay():
    return jnp.log1p(-jnp.exp2(-5.0 - jnp.arange(N_HEADS, dtype=F32)))


def _rope_tables(pos, half, reps):
    inv = ROPE_BASE ** (-jnp.arange(half, dtype=F32) / half)
    ang = pos.astype(F32)[:, None] * inv[None, :]
    cos, sin = jnp.cos(ang), jnp.sin(ang)
    cos_t = jnp.tile(jnp.concatenate([cos, cos], axis=1), (1, reps))
    sin_t = jnp.tile(jnp.concatenate([-sin, sin], axis=1), (1, reps))
    return cos_t, sin_t


def _ret_norm_gate(outs, g):
    o = jnp.concatenate([x * lax.rsqrt(jnp.mean(x * x, axis=-1, keepdims=True) + NORM_EPS) for x in outs],
                        axis=-1)
    return (g * _sigmoid(g)) * o


def _ret_chunk_kernel(pb_ref, cos_ref, sin_ref, dmask_ref, dq_ref, dk_ref, dc_ref, o_ref, sfin_ref, s_scr):
    c = pl.program_id(1)

    @pl.when(c == 0)
    def _():
        s_scr[...] = jnp.zeros_like(s_scr)

    pb = pb_ref[...]
    cos, sin = cos_ref[...], sin_ref[...]
    q = _rotate(pb[:, 0:256], cos, sin, 32, 256)
    k = _rotate(pb[:, 256:512], cos, sin, 32, 256) * (HEAD_DIM ** -0.5)
    v = pb[:, 512:768]
    qd = q * dq_ref[...]
    kd = k * dk_ref[...]
    dc = dc_ref[...]
    outs = []
    for h in range(N_HEADS):
        sl = slice(h * HEAD_DIM, (h + 1) * HEAD_DIM)
        qk = _bdot_nt(q[:, sl], k[:, sl]) * dmask_ref[h]
        s = s_scr[h]
        outs.append(_bdot(qk, v[:, sl]) + _bdot(qd[:, sl], s))
        s_scr[h] = s * dc[h] + _bdot_tn(kd[:, sl], v[:, sl])
    o_ref[...] = _ret_norm_gate(outs, pb[:, 768:1024])

    @pl.when(c == pl.num_programs(1) - 1)
    def _():
        sfin_ref[...] = s_scr[...]


def _ret_prompt(p, nseq, seq_len):
    lg = _ret_log_decay()
    t = jnp.arange(RET_CHUNK, dtype=F32)
    diff = t[:, None] - t[None, :]
    dmask = jnp.where(diff >= 0, jnp.exp(lg[:, None, None] * jnp.maximum(diff, 0.0)), 0.0)
    lanes = lambda a: jnp.repeat(a, HEAD_DIM, axis=-1)
    dq = lanes(jnp.exp(lg[None, :] * (t[:, None] + 1.0)))
    dk = lanes(jnp.exp(lg[None, :] * (RET_CHUNK - 1.0 - t[:, None])))
    dc = jnp.broadcast_to(jnp.exp(lg * RET_CHUNK)[:, None, None], (N_HEADS, 1, HEAD_DIM))
    cos, sin = _rope_tables(jnp.arange(seq_len), 32, N_HEADS)
    nc = seq_len // RET_CHUNK
    full = lambda *s: pl.BlockSpec(s, lambda n, c: (0,) * len(s))
    return pl.pallas_call(
        _ret_chunk_kernel,
        grid=(nseq, nc),
        in_specs=[pl.BlockSpec((RET_CHUNK, 1024), lambda n, c: (n * nc + c, P_RET // 1024)),
                  pl.BlockSpec((RET_CHUNK, 256), lambda n, c: (c, 0)),
                  pl.BlockSpec((RET_CHUNK, 256), lambda n, c: (c, 0)),
                  full(N_HEADS, RET_CHUNK, RET_CHUNK), full(RET_CHUNK, 256), full(RET_CHUNK, 256),
                  full(N_HEADS, 1, HEAD_DIM)],
        out_specs=[pl.BlockSpec((RET_CHUNK, 256), lambda n, c: (n * nc + c, 0)),
                   pl.BlockSpec((None, N_HEADS, HEAD_DIM, HEAD_DIM), lambda n, c: (n, 0, 0, 0))],
        out_shape=[jax.ShapeDtypeStruct((nseq * seq_len, 256), F32),
                   jax.ShapeDtypeStruct((nseq, N_HEADS, HEAD_DIM, HEAD_DIM), F32)],
        scratch_shapes=[pltpu.VMEM((N_HEADS, HEAD_DIM, HEAD_DIM), F32)],
        compiler_params=_cparams(("parallel", "arbitrary")),
        name="retention_chunks",
    )(p, cos, sin, dmask, dq, dk, dc)


def _ret_single_kernel(pb_ref, cos_ref, sin_ref, dec_ref, s_ref, o_ref, so_ref, *, bs):
    pb = pb_ref[...]
    cos, sin = cos_ref[...], sin_ref[...]
    q = _rotate(pb[:, 0:256], cos, sin, 32, 256)
    k = _rotate(pb[:, 256:512], cos, sin, 32, 256) * (HEAD_DIM ** -0.5)
    v = pb[:, 512:768]
    eye = _eye64()
    dec = dec_ref[...]
    rows = []
    for n in range(bs):
        outs = []
        for h in range(N_HEADS):
            sl = slice(h * HEAD_DIM, (h + 1) * HEAD_DIM)
            qr, kr, vr = q[n:n + 1, sl], k[n:n + 1, sl], v[n:n + 1, sl]
            q_col = jnp.sum(jnp.where(eye, qr, 0.0), axis=-1, keepdims=True)
            k_col = jnp.sum(jnp.where(eye, kr, 0.0), axis=-1, keepdims=True)
            s = s_ref[n, h]
            d = dec[h]
            o = jnp.sum(qr * kr, axis=-1, keepdims=True) * vr + jnp.sum((q_col * d) * s, axis=0, keepdims=True)
            so_ref[n, h] = s * d + k_col * vr
            outs.append(o)
        rows.append(jnp.concatenate(outs, axis=-1))
    o = jnp.concatenate(rows, axis=0)
    g = pb[:, 768:1024]
    o_ref[...] = _ret_norm_gate([o[:, h * HEAD_DIM:(h + 1) * HEAD_DIM] for h in range(N_HEADS)], g)


def _ret_single(p, s0, pos, bs=8):
    nb = s0.shape[0]
    cos, sin = _rope_tables(jnp.full((1,), pos), 32, N_HEADS)
    dec = jnp.broadcast_to(jnp.exp(_ret_log_decay())[:, None, None], (N_HEADS, 1, HEAD_DIM))
    s_spec = pl.BlockSpec((bs, N_HEADS, HEAD_DIM, HEAD_DIM), lambda i: (i, 0, 0, 0))
    return pl.pallas_call(
        functools.partial(_ret_single_kernel, bs=bs),
        grid=(nb // bs,),
        in_specs=[pl.BlockSpec((bs, 1024), lambda i: (i, P_RET // 1024)),
                  pl.BlockSpec((1, 256), lambda i: (0, 0)), pl.BlockSpec((1, 256), lambda i: (0, 0)),
                  pl.BlockSpec((N_HEADS, 1, HEAD_DIM), lambda i: (0, 0, 0)), s_spec],
        out_specs=[pl.BlockSpec((bs, 256), lambda i: (i, 0)), s_spec],
        out_shape=[jax.ShapeDtypeStruct((nb, 256), F32), jax.ShapeDtypeStruct(s0.shape, F32)],
        compiler_params=_cparams(("parallel",)),
        name="retention_single",
    )(p, cos, sin, dec, s0)


def _mla_prep_kernel(cq_ref, ckv_ref, kr_ref, cos_ref, sin_ref, qng_ref, kvg_ref, wn_ref, wr_ref, wuk_ref,
                     qf_ref, rows_ref):
    cqn = _rms(cq_ref[...], qng_ref[...], n=Q_LORA)
    q_nope = _bdot(cqn, wn_ref[...]) * MLA_SCALE
    q_rope = _bdot(cqn, wr_ref[...]) * MLA_SCALE
    q_lat = _bdot(q_nope, wuk_ref[...])
    cos, sin = cos_ref[...], sin_ref[...]
    q_rope = _rotate(q_rope, cos, sin, 16, 128)
    kr = _rotate(kr_ref[...], cos, sin, 16, 128)
    lane = lax.broadcasted_iota(jnp.int32, kr.shape, 1)
    keep = lane < ROPE_DIM
    rows_ref[:, 0:KV_LORA] = _rms(ckv_ref[...], kvg_ref[...])
    rows_ref[:, KV_LORA:MLA_WP] = jnp.where(keep, kr, 0.0)
    for h in range(N_HEADS):
        qf_ref[h, :, 0:KV_LORA] = q_lat[:, h * KV_LORA:(h + 1) * KV_LORA]
        shifted = q_rope if h == 0 else pltpu.roll(q_rope, 128 - h * ROPE_DIM, 1)
        qf_ref[h, :, KV_LORA:MLA_WP] = jnp.where(keep, shifted, 0.0)


def _mla_prep(p, lw, pos, tm, tiles_per_seq):
    rows = p.shape[0]
    cos, sin = _rope_tables(pos, 16, 4)
    if pos.shape[0] == 1:
        tab = pl.BlockSpec((1, 128), lambda i: (0, 0))
    else:
        tab = pl.BlockSpec((tm, 128), lambda i: (i % tiles_per_seq, 0))
    full = lambda a, b: pl.BlockSpec((a, b), lambda i: (0, 0))
    return pl.pallas_call(
        _mla_prep_kernel,
        grid=(rows // tm,),
        in_specs=[pl.BlockSpec((tm, 256), lambda i: (i, P_CQ // 256)),
                  pl.BlockSpec((tm, 256), lambda i: (i, P_CKV // 256)),
                  pl.BlockSpec((tm, 128), lambda i: (i, P_KRFL // 128)),
                  tab, tab, full(1, 256), full(1, 256), full(256, 256), full(256, 128), full(256, 1024)],
        out_specs=[pl.BlockSpec((N_HEADS, tm, MLA_WP), lambda i: (0, i, 0)),
                   pl.BlockSpec((tm, MLA_WP), lambda i: (i, 0))],
        out_shape=[jax.ShapeDtypeStruct((N_HEADS, rows, MLA_WP), F32),
                   jax.ShapeDtypeStruct((rows, MLA_WP), F32)],
        compiler_params=_cparams(("parallel",)),
        name="mla_prep",
    )(p, p, p, cos, sin, lw["qn_g"], lw["kvn_g"], lw["w_nope"], lw["w_rope"], lw["wuk_bd"])


def _tri_schedule(n):
    qi = np.concatenate([np.full((i + 1,), i, np.int32) for i in range(n)])
    kj = np.concatenate([np.arange(i + 1, dtype=np.int32) for i in range(n)])
    return jnp.asarray(qi), jnp.asarray(kj)


def _online_softmax(s, m_ref, l_ref):
    m_prev = m_ref[...]
    m_new = jnp.maximum(m_prev, jnp.max(s, axis=-1, keepdims=True))
    alpha = jnp.exp(m_prev - m_new)
    p = jnp.exp(s - m_new)
    l_ref[...] = alpha * l_ref[...] + jnp.sum(p, axis=-1, keepdims=True)
    m_ref[...] = m_new
    return p, alpha


def _mla_flash_kernel(qi_ref, kj_ref, q_ref, kv_ref, wuv_ref, o_ref, m_scr, l_scr, acc_scr, *, tq):
    step = pl.program_id(1)
    i, j = qi_ref[step], kj_ref[step]

    @pl.when(j == 0)
    def _():
        m_scr[...] = jnp.full_like(m_scr, NEG)
        l_scr[...] = jnp.zeros_like(l_scr)
        acc_scr[...] = jnp.zeros_like(acc_scr)

    q = q_ref[...].reshape(N_HEADS * tq, MLA_WP)
    kv = kv_ref[...].astype(BF16)
    s = _bdot_nt(q, kv)
    row = lax.broadcasted_iota(jnp.int32, s.shape, 0) % tq
    col = lax.broadcasted_iota(jnp.int32, s.shape, 1)
    s = jnp.where((j < i) | (col <= row), s, NEG)
    p, alpha = _online_softmax(s, m_scr, l_scr)
    acc_scr[...] = alpha * acc_scr[...] + jnp.dot(p.astype(BF16), kv[:, 0:KV_LORA], preferred_element_type=F32)

    @pl.when(j == i)
    def _():
        o = acc_scr[...] / l_scr[...]
        o_ref[...] = jnp.concatenate(
            [_bdot(o[h * tq:(h + 1) * tq], wuv_ref[h]) for h in range(N_HEADS)], axis=-1)


def _mla_flash(qf, rows, wuv, nseq, seq_len, tq):
    nq = seq_len // tq
    qi, kj = _tri_schedule(nq)
    grid_spec = pltpu.PrefetchScalarGridSpec(
        num_scalar_prefetch=2,
        grid=(nseq, int(qi.shape[0])),
        in_specs=[pl.BlockSpec((N_HEADS, tq, MLA_WP), lambda n, s, qi, kj: (0, n * nq + qi[s], 0)),
                  pl.BlockSpec((tq, MLA_WP), lambda n, s, qi, kj: (n * nq + kj[s], 0)),
                  pl.BlockSpec((N_HEADS, KV_LORA, HEAD_DIM), lambda n, s, qi, kj: (0, 0, 0))],
        out_specs=pl.BlockSpec((tq, 256), lambda n, s, qi, kj: (n * nq + qi[s], 0)),
        scratch_shapes=[pltpu.VMEM((N_HEADS * tq, 1), F32), pltpu.VMEM((N_HEADS * tq, 1), F32),
                        pltpu.VMEM((N_HEADS * tq, KV_LORA), F32)])
    return pl.pallas_call(
        functools.partial(_mla_flash_kernel, tq=tq),
        grid_spec=grid_spec,
        out_shape=jax.ShapeDtypeStruct((nseq * seq_len, 256), F32),
        compiler_params=_cparams(("parallel", "arbitrary")),
        name="mla_prompt_attention",
    )(qi, kj, qf, rows, wuv)


def _mla_decode_kernel(pt_ref, q_ref, new_ref, wuv_ref, *rest, pg):
    pages = rest[:pg]
    o_ref = rest[pg]
    m_scr, l_scr, acc_scr = rest[pg + 1:]
    j = pl.program_id(1)

    @pl.when(j == 0)
    def _():
        m_scr[...] = jnp.full_like(m_scr, NEG)
        l_scr[...] = jnp.zeros_like(l_scr)
        acc_scr[...] = jnp.zeros_like(acc_scr)

    q = q_ref[...].astype(BF16)
    q_lat, q_rope = q[:, 0:KV_LORA], q[:, KV_LORA:MLA_W]
    kts = [pg_ref[...].astype(BF16) for pg_ref in pages]
    s = jnp.concatenate(
        [jnp.dot(q_lat, kt[0:KV_LORA], preferred_element_type=F32)
         + jnp.dot(q_rope, kt[KV_LORA:MLA_W], preferred_element_type=F32)
         for kt in kts], axis=-1)
    p, alpha = _online_softmax(s, m_scr, l_scr)
    p = p.astype(BF16)
    acc = alpha * acc_scr[...]
    for i, kt in enumerate(kts):
        acc = acc + lax.dot_general(p[:, i * PAGE:(i + 1) * PAGE], kt[0:KV_LORA], _NT,
                                    preferred_element_type=F32)
    acc_scr[...] = acc

    @pl.when(j == pl.num_programs(1) - 1)
    def _():
        new = new_ref[...]
        s_new = jnp.sum(q_ref[...] * new, axis=-1, keepdims=True)
        m_prev = m_scr[...]
        m_new = jnp.maximum(m_prev, s_new)
        a = jnp.exp(m_prev - m_new)
        p_new = jnp.exp(s_new - m_new)
        l = a * l_scr[...] + p_new
        o = (a * acc_scr[...] + p_new * new[:, 0:KV_LORA]) / l
        o_ref[...] = jnp.concatenate([_bdot(o[h:h + 1], wuv_ref[h]) for h in range(N_HEADS)], axis=-1)


def _mla_decode(cache, layer, page_table, q, new_rows, wuv, pg):
    nb, n_pages = page_table.shape
    page_spec = lambda i: pl.BlockSpec((None, None, MLA_W, PAGE),
                                       lambda b, j, pt, i=i: (layer, pt[b, j * pg + i], 0, 0))
    grid_spec = pltpu.PrefetchScalarGridSpec(
        num_scalar_prefetch=1,
        grid=(nb, n_pages // pg),
        in_specs=[pl.BlockSpec((None, 8, MLA_WP), lambda b, j, pt: (b, 0, 0)),
                  pl.BlockSpec((None, 1, MLA_WP), lambda b, j, pt: (b, 0, 0)),
                  pl.BlockSpec((N_HEADS, KV_LORA, HEAD_DIM), lambda b, j, pt: (0, 0, 0))]
                 + [page_spec(i) for i in range(pg)],
        out_specs=pl.BlockSpec((None, 1, 256), lambda b, j, pt: (b, 0, 0)),
        scratch_shapes=[pltpu.VMEM((8, 1), F32), pltpu.VMEM((8, 1), F32), pltpu.VMEM((8, KV_LORA), F32)])
    out = pl.pallas_call(
        functools.partial(_mla_decode_kernel, pg=pg),
        grid_spec=grid_spec,
        out_shape=jax.ShapeDtypeStruct((nb, 1, 256), F32),
        compiler_params=_cparams(("parallel", "arbitrary")),
        name="mla_paged_attention",
    )(page_table, q, new_rows, wuv, *([cache] * pg))
    return out.reshape(nb, 256)


def _log_sigmoid(x):
    return jnp.minimum(x, 0.0) - jnp.log(1.0 + jnp.exp(-jnp.abs(x)))


def _fox_prep_kernel(fq_ref, fk_ref, fv_ref, fl_ref, bf_ref, tri_ref, q_o, k_o, v_o, lf_o, c_o, ct_o, carry,
                     *, cumulative):
    fq = fq_ref[...] * FOX_SCALE
    fk, fv = fk_ref[...], fv_ref[...]
    for h in range(N_HEADS):
        q_o[h] = fq[:, h * HEAD_DIM:(h + 1) * HEAD_DIM]
    for g in range(2):
        k_o[g] = fk[:, g * HEAD_DIM:(g + 1) * HEAD_DIM]
        v_o[g] = fv[:, g * HEAD_DIM:(g + 1) * HEAD_DIM]
    x = pltpu.roll(fl_ref[...] + bf_ref[...], 128 - ROPE_DIM, 1)
    lane = lax.broadcasted_iota(jnp.int32, x.shape, 1)
    lf = jnp.where(lane < N_HEADS, _log_sigmoid(x), 0.0)
    lf_o[...] = lf
    if cumulative:
        @pl.when(pl.program_id(1) == 0)
        def _():
            carry[...] = jnp.zeros_like(carry)
        c = _dot01_left(tri_ref[...], lf) + carry[...]
        carry[...] = c[c.shape[0] - 1:c.shape[0], :]
    else:
        c = lf
    c_o[...] = c
    ct_o[...] = jnp.transpose(c)[0:8, :]


def _fox_prep(p, bf, tm, nseq, seq_len, cumulative):
    rows = p.shape[0]
    tps = seq_len // tm if cumulative else rows // tm
    gn = nseq if cumulative else 1
    tri = (jnp.arange(tm)[:, None] >= jnp.arange(tm)[None, :]).astype(BF16)
    idx = lambda w, off: (lambda n, i: (n * tps + i, off // w))
    out_idx = lambda n, i: (n * tps + i, 0)
    hm = lambda k: pl.BlockSpec((k, tm, HEAD_DIM), lambda n, i: (0, n * tps + i, 0))
    return pl.pallas_call(
        functools.partial(_fox_prep_kernel, cumulative=cumulative),
        grid=(gn, tps),
        in_specs=[pl.BlockSpec((tm, 256), idx(256, P_FQ)), pl.BlockSpec((tm, 128), idx(128, P_FK)),
                  pl.BlockSpec((tm, 128), idx(128, P_FV)), pl.BlockSpec((tm, 128), idx(128, P_KRFL)),
                  pl.BlockSpec((1, 128), lambda n, i: (0, 0)), pl.BlockSpec((tm, tm), lambda n, i: (0, 0))],
        out_specs=[hm(4), hm(2), hm(2), pl.BlockSpec((tm, 128), out_idx), pl.BlockSpec((tm, 128), out_idx),
                   pl.BlockSpec((8, tm), lambda n, i: (0, n * tps + i))],
        out_shape=[jax.ShapeDtypeStruct((4, rows, HEAD_DIM), F32), jax.ShapeDtypeStruct((2, rows, HEAD_DIM), F32),
                   jax.ShapeDtypeStruct((2, rows, HEAD_DIM), F32), jax.ShapeDtypeStruct((rows, 128), F32),
                   jax.ShapeDtypeStruct((rows, 128), F32), jax.ShapeDtypeStruct((8, rows), F32)],
        scratch_shapes=[pltpu.VMEM((1, 128), F32)],
        compiler_params=_cparams(("parallel", "arbitrary")),
        name="fox_prep",
    )(p, p, p, p, bf, tri)


def _fox_flash_kernel(qi_ref, kj_ref, q_ref, k_ref, v_ref, c_ref, ct_ref, o_ref, m_scr, l_scr, acc_scr, *, tq):
    step = pl.program_id(1)
    i, j = qi_ref[step], kj_ref[step]

    @pl.when(j == 0)
    def _():
        m_scr[...] = jnp.full_like(m_scr, NEG)
        l_scr[...] = jnp.zeros_like(l_scr)
        acc_scr[...] = jnp.zeros_like(acc_scr)

    row = lax.broadcasted_iota(jnp.int32, (tq, tq), 0)
    col = lax.broadcasted_iota(jnp.int32, (tq, tq), 1)
    visible = (j < i) | (col <= row)
    c = c_ref[...]
    for h in range(N_HEADS):
        g = h // 2
        s = _bdot_nt(q_ref[h], k_ref[g]) + (c[:, h:h + 1] - ct_ref[h:h + 1, :])
        s = jnp.where(visible, s, NEG)
        p, alpha = _online_softmax(s, m_scr.at[h], l_scr.at[h])
        acc_scr[h] = alpha * acc_scr[h] + _bdot(p, v_ref[g])

    @pl.when(j == i)
    def _():
        o_ref[...] = jnp.concatenate([acc_scr[h] / l_scr[h] for h in range(N_HEADS)], axis=-1)


def _fox_flash(q, k, v, c, ct, nseq, seq_len, tq):
    nq = seq_len // tq
    qi, kj = _tri_schedule(nq)
    qmap = lambda n, s, qi, kj: (0, n * nq + qi[s], 0)
    kmap = lambda n, s, qi, kj: (0, n * nq + kj[s], 0)
    grid_spec = pltpu.PrefetchScalarGridSpec(
        num_scalar_prefetch=2,
        grid=(nseq, int(qi.shape[0])),
        in_specs=[pl.BlockSpec((4, tq, HEAD_DIM), qmap), pl.BlockSpec((2, tq, HEAD_DIM), kmap),
                  pl.BlockSpec((2, tq, HEAD_DIM), kmap),
                  pl.BlockSpec((tq, 128), lambda n, s, qi, kj: (n * nq + qi[s], 0)),
                  pl.BlockSpec((8, tq), lambda n, s, qi, kj: (0, n * nq + kj[s]))],
        out_specs=pl.BlockSpec((tq, 256), lambda n, s, qi, kj: (n * nq + qi[s], 0)),
        scratch_shapes=[pltpu.VMEM((4, tq, 1), F32), pltpu.VMEM((4, tq, 1), F32),
                        pltpu.VMEM((4, tq, HEAD_DIM), F32)])
    return pl.pallas_call(
        functools.partial(_fox_flash_kernel, tq=tq),
        grid_spec=grid_spec,
        out_shape=jax.ShapeDtypeStruct((nseq * seq_len, 256), F32),
        compiler_params=_cparams(("parallel", "arbitrary")),
        name="fox_prompt_attention",
    )(qi, kj, q, k, v, c, ct)


def _fox_decode_kernel(pt_ref, q_ref, kn_ref, vn_ref, cn_ref, slt_ref, *rest, pg):
    kps, vps, lps = rest[:pg], rest[pg:2 * pg], rest[2 * pg:3 * pg]
    o_ref = rest[3 * pg]
    m_scr, l_scr, acc_scr, run_scr, lf_scr = rest[3 * pg + 1:]
    j = pl.program_id(1)

    @pl.when(j == 0)
    def _():
        m_scr[...] = jnp.full_like(m_scr, NEG)
        l_scr[...] = jnp.zeros_like(l_scr)
        acc_scr[...] = jnp.zeros_like(acc_scr)
        run_scr[...] = jnp.zeros_like(run_scr)
        lf_scr[...] = jnp.zeros_like(lf_scr)

    for i in range(pg):
        lf_scr[i * 8:i * 8 + N_HEADS, :] = lps[i][...]
    lf = lf_scr[...]
    within = _dot01(lf, slt_ref[...])
    totals = jnp.sum(lf, axis=-1, keepdims=True)
    q = q_ref[...].astype(BF16)
    run = run_scr[...] + cn_ref[...]
    tiles = []
    for i in range(pg):
        kt = kps[i][...].reshape(2 * HEAD_DIM, PAGE).astype(BF16)
        s = jnp.dot(q, kt, preferred_element_type=F32)
        tiles.append(s + within[i * 8:(i + 1) * 8, :] + run)
        run = run + totals[i * 8:(i + 1) * 8, :]
    run_scr[...] = run - cn_ref[...]
    s = jnp.concatenate(tiles, axis=-1)
    p, alpha = _online_softmax(s, m_scr, l_scr)
    p = p.astype(BF16)
    acc = alpha * acc_scr[...]
    for i in range(pg):
        vt = vps[i][...].reshape(2 * HEAD_DIM, PAGE).astype(BF16)
        acc = acc + lax.dot_general(p[:, i * PAGE:(i + 1) * PAGE], vt, _NT, preferred_element_type=F32)
    acc_scr[...] = acc

    @pl.when(j == pl.num_programs(1) - 1)
    def _():
        s_new = jnp.sum(q_ref[...] * kn_ref[...], axis=-1, keepdims=True)
        m_prev = m_scr[...]
        m_new = jnp.maximum(m_prev, s_new)
        a = jnp.exp(m_prev - m_new)
        p_new = jnp.exp(s_new - m_new)
        l = a * l_scr[...] + p_new
        o = (a * acc_scr[...] + p_new * vn_ref[...]) / l
        o_ref[...] = jnp.concatenate(
            [o[h:h + 1, (h // 2) * HEAD_DIM:(h // 2 + 1) * HEAD_DIM] for h in range(N_HEADS)], axis=-1)


def _fox_decode(cache_k, cache_v, cache_lf, layer, page_table, q8, k_new, v_new, cn8, pg):
    nb, n_pages = page_table.shape
    slt = (jnp.arange(PAGE)[:, None] > jnp.arange(PAGE)[None, :]).astype(BF16)
    newest_first = lambda b, j, pt, i: pt[b, n_pages - 1 - (j * pg + i)]
    kv_spec = lambda i: pl.BlockSpec((None, None, 2, HEAD_DIM, PAGE),
                                     lambda b, j, pt, i=i: (layer, newest_first(b, j, pt, i), 0, 0, 0))
    lf_spec = lambda i: pl.BlockSpec((None, None, N_HEADS, PAGE),
                                     lambda b, j, pt, i=i: (layer, newest_first(b, j, pt, i), 0, 0))
    per_seq = lambda r, w: pl.BlockSpec((None, r, w), lambda b, j, pt: (b, 0, 0))
    grid_spec = pltpu.PrefetchScalarGridSpec(
        num_scalar_prefetch=1,
        grid=(nb, n_pages // pg),
        in_specs=[per_seq(8, 128), per_seq(1, 128), per_seq(1, 128), per_seq(8, 1),
                  pl.BlockSpec((PAGE, PAGE), lambda b, j, pt: (0, 0))]
                 + [kv_spec(i) for i in range(pg)] + [kv_spec(i) for i in range(pg)]
                 + [lf_spec(i) for i in range(pg)],
        out_specs=pl.BlockSpec((None, 1, 256), lambda b, j, pt: (b, 0, 0)),
        scratch_shapes=[pltpu.VMEM((8, 1), F32), pltpu.VMEM((8, 1), F32), pltpu.VMEM((8, 128), F32),
                        pltpu.VMEM((8, 1), F32), pltpu.VMEM((pg * 8, PAGE), F32)])
    out = pl.pallas_call(
        functools.partial(_fox_decode_kernel, pg=pg),
        grid_spec=grid_spec,
        out_shape=jax.ShapeDtypeStruct((nb, 1, 256), F32),
        compiler_params=_cparams(("parallel", "arbitrary")),
        name="fox_paged_attention",
    )(page_table, q8, k_new, v_new, cn8, slt, *([cache_k] * pg), *([cache_v] * pg), *([cache_lf] * pg))
    return out.reshape(nb, 256)


def _merge_kernel(x_ref, pg_ref, ya_ref, yb_ref, yc_ref, yd_ref, wb_ref, wo_ref, o_ref):
    pg = pg_ref[...]
    merged = None
    for bi, y_ref in enumerate((ya_ref, yb_ref, yc_ref, yd_ref)):
        gate = _sigmoid(pg[:, bi * D_MODEL:(bi + 1) * D_MODEL])
        term = gate * jnp.dot(y_ref[...].astype(BF16), wb_ref[bi], preferred_element_type=F32)
        merged = term if merged is None else merged + term
    o_ref[...] = x_ref[...] + jnp.dot(merged.astype(BF16), wo_ref[...], preferred_element_type=F32)


def _merge(x, p, ya, yb, yc, yd, wb, wo, tm):
    rows = x.shape[0]
    row = lambda w: pl.BlockSpec((tm, w), lambda i: (i, 0))
    return pl.pallas_call(
        _merge_kernel,
        grid=(rows // tm,),
        in_specs=[row(D_MODEL), pl.BlockSpec((tm, 4 * D_MODEL), lambda i: (i, P_GATE)),
                  row(256), row(256), row(256), row(256),
                  pl.BlockSpec((4, BRANCH_W, D_MODEL), lambda i: (0, 0, 0)),
                  pl.BlockSpec((D_MODEL, D_MODEL), lambda i: (0, 0))],
        out_specs=row(D_MODEL),
        out_shape=jax.ShapeDtypeStruct((rows, D_MODEL), F32),
        compiler_params=_cparams(("parallel",)),
        name="gated_merge",
    )(x, p, ya, yb, yc, yd, wb, wo)


def _mlp_kernel(x_ref, g_ref, wu_ref, wd_ref, fg_ref, *o_refs, final):
    x = x_ref[...]
    h = _rms(x, g_ref[...]).astype(BF16)
    u = jnp.maximum(jnp.dot(h, wu_ref[...], preferred_element_type=F32), 0.0)
    y = x + jnp.dot((u * u).astype(BF16), wd_ref[...], preferred_element_type=F32)
    o_refs[0][...] = y
    if final:
        o_refs[1][...] = _rms(y, fg_ref[...])


def _mlp(x, g, wu, wd, fg, tm, final):
    rows = x.shape[0]
    row = pl.BlockSpec((tm, D_MODEL), lambda i: (i, 0))
    vec = pl.BlockSpec((1, D_MODEL), lambda i: (0, 0))
    const = lambda a, b: pl.BlockSpec((a, b), lambda i: (0, 0), pipeline_mode=pl.Buffered(1))
    n_out = 2 if final else 1
    return pl.pallas_call(
        functools.partial(_mlp_kernel, final=final),
        grid=(rows // tm,),
        in_specs=[row, vec, const(D_MODEL, D_FF), const(D_FF, D_MODEL), vec],
        out_specs=[row] * n_out,
        out_shape=[jax.ShapeDtypeStruct((rows, D_MODEL), F32)] * n_out,
        compiler_params=_cparams(("parallel",)),
        name="mlp",
    )(x, g, wu, wd, fg)


def _layer(x, lw, consts, *, nseq, seq_len, tm, decode):
    rows = x.shape[0]
    p = _norm_matmul(x, lw["norm1_g"], lw["w_in"], tm)
    prompt = decode is None

    if prompt:
        r, w, kh, v, kk, b, g = _rwkv_prep(p, None, lw, consts["bd"], tm, True, seq_len)
        y, rwkv_new = _rwkv_scan(r, w, kh, v, kk, b, nseq, seq_len)
    else:
        r, w, kh, v, kk, b, g = _rwkv_prep(p, decode["shift"], lw, consts["bd"], tm, False, seq_len)
        y, rwkv_new = _rwkv_single(r, w, kh, v, kk, b, decode["rwkv"])
    ya = _rwkv_post(y, r, kh, v, g, lw, tm)
    shift_new = p[:, P_RWKV:P_RWKV + RWKV_IN].reshape(nseq, seq_len, RWKV_IN)[:, -1]

    if prompt:
        yb, ret_new = _ret_prompt(p, nseq, seq_len)
    else:
        yb, ret_new = _ret_single(p, decode["ret"], decode["pos"])

    pos = jnp.arange(seq_len) if prompt else jnp.full((1,), decode["pos"])
    qf, mla_rows = _mla_prep(p, lw, pos, tm, max(seq_len // tm, 1))
    if prompt:
        yc = _mla_flash(qf, mla_rows, lw["wuv"], nseq, seq_len, min(ATTN_TILE, seq_len))
    else:
        q8 = jnp.pad(jnp.transpose(qf, (1, 0, 2)), ((0, 0), (0, 8 - N_HEADS), (0, 0)))
        yc = _mla_decode(decode["cache_mla"], decode["layer"], decode["page_table"], q8,
                         mla_rows.reshape(rows, 1, MLA_WP), lw["wuv"], decode["pg_mla"])

    fq, fk, fv, lf, c, ct = _fox_prep(p, lw["bf"], tm, nseq, seq_len, prompt)
    if prompt:
        yd = _fox_flash(fq, fk, fv, c, ct, nseq, seq_len, min(ATTN_TILE, seq_len))
    else:
        q8 = jnp.zeros((rows, 8, 128), F32)
        for h in range(N_HEADS):
            gq = (h // 2) * HEAD_DIM
            q8 = q8.at[:, h, gq:gq + HEAD_DIM].set(fq[h])
        cn8 = jnp.pad(lf[:, 0:N_HEADS], ((0, 0), (0, 8 - N_HEADS))).reshape(rows, 8, 1)
        k_new = p[:, P_FK:P_FK + 128].reshape(rows, 1, 128)
        v_new = p[:, P_FV:P_FV + 128].reshape(rows, 1, 128)
        yd = _fox_decode(decode["cache_k"], decode["cache_v"], decode["cache_lf"], decode["layer"],
                         decode["page_table"], q8, k_new, v_new, cn8, decode["pg_fox"])

    x1 = _merge(x, p, ya, yb, yc, yd, lw["w_branch"], lw["w_out"], tm)
    new = (mla_rows[:, 0:MLA_W], p[:, P_FK:P_FK + 128], p[:, P_FV:P_FV + 128], lf[:, 0:N_HEADS],
           rwkv_new, shift_new, ret_new)
    return x1, new


def _layer_weights(l, norm1_g, norm2_g, w_in, rwkv_mu, rwkv_w0, rwkv_w2, rwkv_a0, rwkv_a2, rwkv_g2, rwkv_kk,
                   rwkv_ka, rwkv_rk, rwkv_ln_g, rwkv_ln_b, mla_qn_g, mla_kvn_g, mla_wuq, mla_wuk, mla_wuv,
                   fox_bf, w_branch, w_out, w_up, w_down):
    wi = w_in[l]
    a, b = wi[:, 0:1024], wi[:, 1024:2048]
    cq, ckv, kr = wi[:, 2048:2240], wi[:, 2240:2496], wi[:, 2496:2528]
    fq, fk, fv, fl = wi[:, 2528:2784], wi[:, 2784:2912], wi[:, 2912:3040], wi[:, 3040:3044]
    gate = wi[:, 3044:7140]
    z = lambda n: jnp.zeros((D_MODEL, n), F32)
    w_all = jnp.concatenate([gate, a, b, fq, fk, fv, cq, z(256 - Q_LORA), ckv, kr, fl, z(128 - ROPE_DIM - 4)],
                            axis=1).astype(BF16)
    row = lambda v: v.reshape(1, -1)
    wuq = jnp.pad(mla_wuq[l], ((0, 256 - Q_LORA), (0, 0), (0, 0)))
    wuk_bd = jnp.zeros((N_HEADS * NOPE_DIM, N_HEADS * KV_LORA), F32)
    for h in range(N_HEADS):
        wuk_bd = wuk_bd.at[h * NOPE_DIM:(h + 1) * NOPE_DIM, h * KV_LORA:(h + 1) * KV_LORA].set(mla_wuk[l][:, h, :].T)
    return dict(
        norm1_g=row(norm1_g[l]), norm2_g=row(norm2_g[l]), w_in=w_all,
        mu=row(rwkv_mu[l]), w0=row(rwkv_w0[l]), w2=rwkv_w2[l].astype(BF16), a0=row(rwkv_a0[l]),
        a2=rwkv_a2[l].astype(BF16), g2=rwkv_g2[l].astype(BF16), kkp=row(rwkv_kk[l]), ka=row(rwkv_ka[l]),
        rk=rwkv_rk[l].reshape(N_HEADS, 1, HEAD_DIM), ln_g=row(rwkv_ln_g[l]), ln_b=row(rwkv_ln_b[l]),
        qn_g=row(jnp.pad(mla_qn_g[l], (0, 256 - Q_LORA))), kvn_g=row(mla_kvn_g[l]),
        w_nope=wuq[:, :, 0:NOPE_DIM].reshape(256, N_HEADS * NOPE_DIM).astype(BF16),
        w_rope=wuq[:, :, NOPE_DIM:].reshape(256, N_HEADS * ROPE_DIM).astype(BF16),
        wuk_bd=wuk_bd.astype(BF16),
        wuv=jnp.transpose(mla_wuv[l], (1, 0, 2)).astype(BF16),
        bf=jnp.pad(fox_bf[l], (ROPE_DIM, 128 - ROPE_DIM - 4)).reshape(1, 128),
        w_branch=w_branch[l].astype(BF16), w_out=w_out[l].astype(BF16),
        w_up=w_up[l].astype(BF16), w_down=w_down[l].astype(BF16))


def kernel(x_prompt, x_sample, cache_mla, cache_fox_k, cache_fox_v, cache_fox_logf, state_rwkv, state_rwkv_shift, state_ret, page_table, norm1_g, norm2_g, final_g, w_in, rwkv_mu, rwkv_w0, rwkv_w2, rwkv_a0, rwkv_a2, rwkv_g2, rwkv_kk, rwkv_ka, rwkv_rk, rwkv_ln_g, rwkv_ln_b, mla_qn_g, mla_kvn_g, mla_wuq, mla_wuk, mla_wuv, fox_bf, w_branch, w_out, w_up, w_down):
    nseq, seq_len = x_prompt.shape[:2]
    nb, dec_len = x_sample.shape[:2]
    assert dec_len == 1
    depth = w_in.shape[0]
    n_pages = page_table.shape[1]
    t_past = n_pages * PAGE
    n_pool = cache_mla.shape[1]
    pg_mla = min(32, n_pages)
    pg_fox = min(32, n_pages)
    tm_p = min(512, seq_len)
    tm_s = nb

    cache_mla_t = jnp.transpose(cache_mla, (0, 1, 3, 2))
    cache_k_t = jnp.transpose(cache_fox_k, (0, 1, 3, 4, 2))
    cache_v_t = jnp.transpose(cache_fox_v, (0, 1, 3, 4, 2))
    cache_lf_t = jnp.transpose(cache_fox_logf, (0, 1, 3, 2))
    consts = dict(bd=(jnp.arange(256)[:, None] // HEAD_DIM == jnp.arange(256)[None, :] // HEAD_DIM).astype(BF16))
    fg = final_g.reshape(1, D_MODEL)

    xp = x_prompt.reshape(nseq * seq_len, D_MODEL)
    xs = x_sample.reshape(nb, D_MODEL)
    new_p, new_s = [], []
    yp = ys = None
    for l in range(depth):
        lw = _layer_weights(l, norm1_g, norm2_g, w_in, rwkv_mu, rwkv_w0, rwkv_w2, rwkv_a0, rwkv_a2, rwkv_g2,
                            rwkv_kk, rwkv_ka, rwkv_rk, rwkv_ln_g, rwkv_ln_b, mla_qn_g, mla_kvn_g, mla_wuq,
                            mla_wuk, mla_wuv, fox_bf, w_branch, w_out, w_up, w_down)
        final = l == depth - 1
        x1, st = _layer(xp, lw, consts, nseq=nseq, seq_len=seq_len, tm=tm_p, decode=None)
        outs = _mlp(x1, lw["norm2_g"], lw["w_up"], lw["w_down"], fg, tm_p, final)
        xp = outs[0]
        if final:
            yp = outs[1]
        new_p.append(st)
        decode = dict(shift=state_rwkv_shift[l], rwkv=state_rwkv[l], ret=state_ret[l], pos=t_past, layer=l,
                      page_table=page_table, cache_mla=cache_mla_t, cache_k=cache_k_t, cache_v=cache_v_t,
                      cache_lf=cache_lf_t, pg_mla=pg_mla, pg_fox=pg_fox)
        x1, st = _layer(xs, lw, consts, nseq=nb, seq_len=1, tm=tm_s, decode=decode)
        outs = _mlp(x1, lw["norm2_g"], lw["w_up"], lw["w_down"], fg, tm_s, final)
        xs = outs[0]
        if final:
            ys = outs[1]
        new_s.append(st)

    def stack(new, i, shape):
        return jnp.stack([st[i] for st in new]).reshape((depth,) + shape)

    res = [yp.reshape(nseq, seq_len, D_MODEL), ys.reshape(nb, 1, D_MODEL)]
    shapes_p = [(nseq, seq_len, MLA_W), (nseq, seq_len, 2, HEAD_DIM), (nseq, seq_len, 2, HEAD_DIM),
                (nseq, seq_len, N_HEADS), (nseq, N_HEADS, HEAD_DIM, HEAD_DIM), (nseq, RWKV_IN),
                (nseq, N_HEADS, HEAD_DIM, HEAD_DIM)]
    shapes_s = [(nb, 1, MLA_W), (nb, 1, 2, HEAD_DIM), (nb, 1, 2, HEAD_DIM), (nb, 1, N_HEADS),
                (nb, N_HEADS, HEAD_DIM, HEAD_DIM), (nb, RWKV_IN), (nb, N_HEADS, HEAD_DIM, HEAD_DIM)]
    for i in range(7):
        res.append(stack(new_p, i, shapes_p[i]))
        res.append(stack(new_s, i, shapes_s[i]))
    return tuple(res)
```

```python
import functools

import jax
import jax.numpy as jnp
import numpy as np
from jax import lax
from jax.experimental import pallas as pl
from jax.experimental.pallas import tpu as pltpu

F32 = jnp.float32
BF16 = jnp.bfloat16

D_MODEL = 1024
HEAD_DIM = 64
N_HEADS = 4
BRANCH_W = 256
PAGE = 128
DECAY_LORA = 64
ICLR_LORA = 64
GATE_LORA = 128
RWKV_IN = 1024
DECAY_SCALE = 0.6065306597126334
GN_EPS = 64e-5
RET_CHUNK = 128
Q_LORA = 192
KV_LORA = 256
NOPE_DIM = 64
ROPE_DIM = 32
MLA_W = KV_LORA + ROPE_DIM
MLA_WP = 384
MLA_SCALE = (NOPE_DIM + ROPE_DIM) ** -0.5
FOX_SCALE = HEAD_DIM ** -0.5
D_FF = 4096
ROPE_BASE = 10000.0
NORM_EPS = 1e-6
NEG = -1e30

P_GATE = 0
P_RWKV = 4096
P_RET = 5120
P_FQ = 6144
P_FK = 6400
P_FV = 6528
P_CQ = 6656
P_CKV = 6912
P_KRFL = 7168
P_W = 7296
P_TN = 2432

VMEM_LIMIT = 56 * 1024 * 1024
ATTN_TILE = 512


def _cparams(sem):
    return pltpu.CompilerParams(dimension_semantics=sem, vmem_limit_bytes=VMEM_LIMIT)


def _bdot(a, b):
    return jnp.dot(a.astype(BF16), b.astype(BF16), preferred_element_type=F32)


def _bdot_nt(a, b):
    return lax.dot_general(a.astype(BF16), b.astype(BF16), (((1,), (1,)), ((), ())),
                           preferred_element_type=F32)


def _bdot_tn(a, b):
    return lax.dot_general(a.astype(BF16), b.astype(BF16), (((0,), (0,)), ((), ())),
                           preferred_element_type=F32)


def _split3(a):
    a1 = a.astype(BF16)
    r1 = a - a1.astype(F32)
    a2 = r1.astype(BF16)
    a3 = (r1 - a2.astype(F32)).astype(BF16)
    return a1, a2, a3


def _dot01(a, ones01):
    a1, a2, a3 = _split3(a)
    d = lambda p: jnp.dot(p, ones01, preferred_element_type=F32)
    return d(a1) + (d(a2) + d(a3))


def _dot01_left(ones01, a):
    a1, a2, a3 = _split3(a)
    d = lambda p: jnp.dot(ones01, p, preferred_element_type=F32)
    return d(a1) + (d(a2) + d(a3))


def _sigmoid(x):
    return 1.0 / (1.0 + jnp.exp(-x))


def _rms(x, g, n=None):
    n = x.shape[-1] if n is None else n
    ms = jnp.sum(x * x, axis=-1, keepdims=True) * (1.0 / n)
    return x * lax.rsqrt(ms + NORM_EPS) * g


def _rotate(x, cos, sin_signed, half, width):
    lane = lax.broadcasted_iota(jnp.int32, x.shape, x.ndim - 1)
    first = (lane % (2 * half)) < half
    partner = jnp.where(first, pltpu.roll(x, width - half, x.ndim - 1), pltpu.roll(x, half, x.ndim - 1))
    return x * cos + partner * sin_signed


def _norm_matmul_kernel(x_ref, g_ref, w_ref, o_ref):
    h = _rms(x_ref[...], g_ref[...])
    o_ref[...] = jnp.dot(h.astype(BF16), w_ref[...], preferred_element_type=F32)


def _norm_matmul(x, g, w, tm):
    rows = x.shape[0]
    return pl.pallas_call(
        _norm_matmul_kernel,
        grid=(P_W // P_TN, rows // tm),
        in_specs=[pl.BlockSpec((tm, D_MODEL), lambda j, i: (i, 0)),
                  pl.BlockSpec((1, D_MODEL), lambda j, i: (0, 0)),
                  pl.BlockSpec((D_MODEL, P_TN), lambda j, i: (0, j))],
        out_specs=pl.BlockSpec((tm, P_TN), lambda j, i: (i, j)),
        out_shape=jax.ShapeDtypeStruct((rows, P_W), F32),
        compiler_params=_cparams(("parallel", "parallel")),
        name="norm_in_proj",
    )(x, g, w)


def _rwkv_prep_kernel(pa_ref, prev_ref, mu_ref, w0_ref, w2_ref, a0_ref, a2_ref, g2_ref, kkp_ref,
                      ka_ref, bd_ref, r_o, w_o, kh_o, v_o, kk_o, b_o, g_o, *, shifted, tiles_per_seq):
    pa = pa_ref[...]
    if shifted:
        i = pl.program_id(0)
        rolled = pltpu.roll(pa, 1, 0)
        halo = prev_ref[7:8, :]
        halo = jnp.where(i % tiles_per_seq == 0, jnp.zeros_like(halo), halo)
        row = lax.broadcasted_iota(jnp.int32, pa.shape, 0)
        prev = jnp.where(row == 0, halo, rolled)
    else:
        prev = prev_ref[...]
    xm = pa + (prev - pa) * mu_ref[...]
    r = xm[:, 0:256]
    k = xm[:, 256:512]
    v = xm[:, 512:768]
    wl = xm[:, 768:832]
    al = xm[:, 832:896]
    gl = xm[:, 896:1024]
    logw = -DECAY_SCALE * _sigmoid(w0_ref[...] + _bdot(jnp.tanh(wl), w2_ref[...]))
    a = _sigmoid(a0_ref[...] + _bdot(al, a2_ref[...]))
    g = _bdot(_sigmoid(gl), g2_ref[...])
    kk = k * kkp_ref[...]
    ss = _dot01(kk * kk, bd_ref[...])
    kk = kk / jnp.maximum(jnp.sqrt(ss), 1e-12)
    kh = k * (1.0 + (a - 1.0) * ka_ref[...])
    b = kk * a
    for h in range(N_HEADS):
        sl = slice(h * HEAD_DIM, (h + 1) * HEAD_DIM)
        r_o[h] = r[:, sl]
        w_o[h] = logw[:, sl]
        kh_o[h] = kh[:, sl]
        v_o[h] = v[:, sl]
        kk_o[h] = kk[:, sl]
        b_o[h] = b[:, sl]
    g_o[...] = g


def _rwkv_prep(p, prev, lw, bd, tm, shifted, seq_len):
    rows = p.shape[0]
    tiles_per_seq = max(seq_len // tm, 1)
    col = P_RWKV // RWKV_IN
    if shifted:
        prev_arr = p
        prev_spec = pl.BlockSpec((8, RWKV_IN), lambda i: (jnp.maximum(i * (tm // 8) - 1, 0), col))
    else:
        prev_arr = prev
        prev_spec = pl.BlockSpec((tm, RWKV_IN), lambda i: (i, 0))
    vec = lambda n: pl.BlockSpec((1, n), lambda i: (0, 0))
    mat = lambda a, b: pl.BlockSpec((a, b), lambda i: (0, 0))
    hm = jax.ShapeDtypeStruct((N_HEADS, rows, HEAD_DIM), F32)
    hm_spec = pl.BlockSpec((N_HEADS, tm, HEAD_DIM), lambda i: (0, i, 0))
    return pl.pallas_call(
        functools.partial(_rwkv_prep_kernel, shifted=shifted, tiles_per_seq=tiles_per_seq),
        grid=(rows // tm,),
        in_specs=[pl.BlockSpec((tm, RWKV_IN), lambda i: (i, col)), prev_spec,
                  vec(RWKV_IN), vec(256), mat(DECAY_LORA, 256), vec(256), mat(ICLR_LORA, 256),
                  mat(GATE_LORA, 256), vec(256), vec(256), mat(256, 256)],
        out_specs=[hm_spec] * 6 + [pl.BlockSpec((tm, 256), lambda i: (i, 0))],
        out_shape=[hm] * 6 + [jax.ShapeDtypeStruct((rows, 256), F32)],
        compiler_params=_cparams(("parallel",)),
        name="rwkv_prep",
    )(p, prev_arr, lw["mu"], lw["w0"], lw["w2"], lw["a0"], lw["a2"], lw["g2"], lw["kkp"], lw["ka"], bd)


def _rwkv_step(s, r, w, kh, v, kk, b, eye):
    sa = jnp.sum(s * kk, axis=-1, keepdims=True)
    v_col = jnp.sum(jnp.where(eye, v, 0.0), axis=-1, keepdims=True)
    s = s * w - sa * b + v_col * kh
    y_col = jnp.sum(s * r, axis=-1, keepdims=True)
    y_row = jnp.sum(jnp.where(eye, y_col, 0.0), axis=0, keepdims=True)
    return s, y_row


def _eye64():
    return (lax.broadcasted_iota(jnp.int32, (HEAD_DIM, HEAD_DIM), 0)
            == lax.broadcasted_iota(jnp.int32, (HEAD_DIM, HEAD_DIM), 1))


def _dot2(a, b, dims=(((1,), (0,)), ((), ()))):
    ah = a.astype(BF16)
    al = (a - ah.astype(F32)).astype(BF16)
    bh = b.astype(BF16)
    bl = (b - bh.astype(F32)).astype(BF16)
    d = lambda x, y: lax.dot_general(x, y, dims, preferred_element_type=F32)
    return d(ah, bh) + (d(ah, bl) + d(al, bh))


_NT = (((1,), (1,)), ((), ()))
_TN = (((0,), (0,)), ((), ()))
RWKV_CHUNK = 64


def _rwkv_chunk_kernel(r_ref, lw_ref, kh_ref, v_ref, kk_ref, b_ref, tri_ref, y_ref, sfin_ref, s_scr):
    c = pl.program_id(1)

    @pl.when(c == 0)
    def _():
        s_scr[...] = jnp.zeros_like(s_scr)

    C = RWKV_CHUNK
    row = lax.broadcasted_iota(jnp.int32, (C, 2 * C), 0)
    col = lax.broadcasted_iota(jnp.int32, (C, 2 * C), 1)
    left = col < C
    colr = jnp.where(left, col, col - C)
    strict = row > colr
    incl = row >= colr
    eye = (row == col)[:, 0:C]
    tri = tri_ref[...]
    heads = range(N_HEADS)
    lw, r, kh, v, kk, b = ([ref[h] for h in heads] for ref in (lw_ref, r_ref, kh_ref, v_ref, kk_ref, b_ref))
    lp = [_dot01_left(tri, lw[h]) for h in heads]
    lp_end = [lp[h][C - 1:C, :] for h in heads]
    kap = [kk[h] * jnp.exp(lp[h] - lw[h]) for h in heads]
    rt = [r[h] * jnp.exp(lp[h]) for h in heads]
    e_in = [jnp.exp(-lp[h]) for h in heads]
    bk = [jnp.concatenate([b[h] * e_in[h], kh[h] * e_in[h]], axis=0) for h in heads]
    rem = [jnp.exp(lp_end[h] - lp[h]) for h in heads]
    g1 = [_dot2(kap[h], bk[h], _NT) for h in heads]
    g2 = [_bdot_nt(rt[h], bk[h]) for h in heads]
    l = [jnp.where(strict[:, 0:C], g1[h][:, 0:C], 0.0) for h in heads]
    x = [jnp.where(eye, 1.0, 0.0) - l[h] for h in heads]
    p = [_dot2(l[h], l[h]) for h in heads]
    for i in range(5):
        x = [x[h] + _dot2(x[h], p[h]) for h in heads]
        if i < 4:
            p = [_dot2(p[h], p[h]) for h in heads]
    s0 = [s_scr[h] for h in heads]
    not_left_strict = jnp.logical_and(jnp.logical_not(left), strict)
    rhs = [_dot2(kap[h], s0[h], _NT)
           + _dot2(jnp.where(not_left_strict, g1[h], 0.0), jnp.concatenate([v[h], v[h]], axis=0)) for h in heads]
    u = [_dot2(x[h], rhs[h]) for h in heads]
    for h in heads:
        g2m = jnp.where(incl, jnp.where(left, -g2[h], g2[h]), 0.0)
        y_ref[h] = _bdot_nt(rt[h], s0[h]) + _bdot(g2m, jnp.concatenate([u[h], v[h]], axis=0))
    for h in heads:
        s_scr[h] = (s0[h] * jnp.exp(lp_end[h]) + _dot2(v[h], kh[h] * rem[h], _TN)
                    - _dot2(u[h], b[h] * rem[h], _TN))

    @pl.when(c == pl.num_programs(1) - 1)
    def _():
        sfin_ref[...] = s_scr[...]


def _rwkv_scan(r, lw, kh, v, kk, b, nseq, seq_len):
    C = RWKV_CHUNK
    view = lambda a: a.reshape(N_HEADS, nseq, seq_len, HEAD_DIM)
    spec = pl.BlockSpec((N_HEADS, None, C, HEAD_DIM), lambda n, c: (0, n, c, 0))
    tri = (jnp.arange(C)[:, None] >= jnp.arange(C)[None, :]).astype(BF16)
    y, sfin = pl.pallas_call(
        _rwkv_chunk_kernel,
        grid=(nseq, seq_len // C),
        in_specs=[spec] * 6 + [pl.BlockSpec((C, C), lambda n, c: (0, 0))],
        out_specs=[spec, pl.BlockSpec((None, N_HEADS, HEAD_DIM, HEAD_DIM), lambda n, c: (n, 0, 0, 0))],
        out_shape=[jax.ShapeDtypeStruct((N_HEADS, nseq, seq_len, HEAD_DIM), F32),
                   jax.ShapeDtypeStruct((nseq, N_HEADS, HEAD_DIM, HEAD_DIM), F32)],
        scratch_shapes=[pltpu.VMEM((N_HEADS, HEAD_DIM, HEAD_DIM), F32)],
        compiler_params=_cparams(("parallel", "arbitrary")),
        name="rwkv_chunks",
    )(view(r), view(lw), view(kh), view(v), view(kk), view(b), tri)
    return y.reshape(N_HEADS, nseq * seq_len, HEAD_DIM), sfin


def _rwkv_single_kernel(r_ref, w_ref, kh_ref, v_ref, kk_ref, b_ref, s_ref, y_ref, so_ref, *, bs):
    eye = _eye64()
    for n in range(bs):
        for h in range(N_HEADS):
            row = lambda ref: ref[h, n:n + 1, :]
            s, y_row = _rwkv_step(s_ref[n, h], row(r_ref), jnp.exp(row(w_ref)), row(kh_ref), row(v_ref),
                                  row(kk_ref), row(b_ref), eye)
            so_ref[n, h] = s
            y_ref[h, n:n + 1, :] = y_row


def _rwkv_single(r, w, kh, v, kk, b, s0, bs=8):
    nb = s0.shape[0]
    spec = pl.BlockSpec((N_HEADS, bs, HEAD_DIM), lambda i: (0, i, 0))
    s_spec = pl.BlockSpec((bs, N_HEADS, HEAD_DIM, HEAD_DIM), lambda i: (i, 0, 0, 0))
    return pl.pallas_call(
        functools.partial(_rwkv_single_kernel, bs=bs),
        grid=(nb // bs,),
        in_specs=[spec] * 6 + [s_spec],
        out_specs=[spec, s_spec],
        out_shape=[jax.ShapeDtypeStruct((N_HEADS, nb, HEAD_DIM), F32),
                   jax.ShapeDtypeStruct(s0.shape, F32)],
        compiler_params=_cparams(("parallel",)),
        name="rwkv_single",
    )(r, w, kh, v, kk, b, s0)


def _rwkv_post_kernel(y_ref, r_ref, kh_ref, v_ref, g_ref, rk_ref, lng_ref, lnb_ref, o_ref):
    outs = []
    for h in range(N_HEADS):
        bonus = jnp.sum(r_ref[h] * kh_ref[h] * rk_ref[h], axis=-1, keepdims=True)
        y = y_ref[h] + bonus * v_ref[h]
        mu = jnp.mean(y, axis=-1, keepdims=True)
        d = y - mu
        var = jnp.mean(d * d, axis=-1, keepdims=True)
        outs.append(d * lax.rsqrt(var + GN_EPS))
    y = jnp.concatenate(outs, axis=-1) * lng_ref[...] + lnb_ref[...]
    o_ref[...] = y * g_ref[...]


def _rwkv_post(y, r, kh, v, g, lw, tm):
    rows = g.shape[0]
    hm_spec = pl.BlockSpec((N_HEADS, tm, HEAD_DIM), lambda i: (0, i, 0))
    vec = pl.BlockSpec((1, 256), lambda i: (0, 0))
    return pl.pallas_call(
        _rwkv_post_kernel,
        grid=(rows // tm,),
        in_specs=[hm_spec] * 4 + [pl.BlockSpec((tm, 256), lambda i: (i, 0)),
                                  pl.BlockSpec((N_HEADS, 1, HEAD_DIM), lambda i: (0, 0, 0)), vec, vec],
        out_specs=pl.BlockSpec((tm, 256), lambda i: (i, 0)),
        out_shape=jax.ShapeDtypeStruct((rows, 256), F32),
        compiler_params=_cparams(("parallel",)),
        name="rwkv_post",
    )(y, r, kh, v, g, lw["rk"], lw["ln_g"], lw["ln_b"])


def _ret_log_decay():
    return jnp.log1p(-jnp.exp2(-5.0 - jnp.arange(N_HEADS, dtype=F32)))


def _rope_tables(pos, half, reps):
    inv = ROPE_BASE ** (-jnp.arange(half, dtype=F32) / half)
    ang = pos.astype(F32)[:, None] * inv[None, :]
    cos, sin = jnp.cos(ang), jnp.sin(ang)
    cos_t = jnp.tile(jnp.concatenate([cos, cos], axis=1), (1, reps))
    sin_t = jnp.tile(jnp.concatenate([-sin, sin], axis=1), (1, reps))
    return cos_t, sin_t


def _ret_norm_gate(outs, g):
    o = jnp.concatenate([x * lax.rsqrt(jnp.mean(x * x, axis=-1, keepdims=True) + NORM_EPS) for x in outs],
                        axis=-1)
    return (g * _sigmoid(g)) * o


def _ret_chunk_kernel(pb_ref, cos_ref, sin_ref, dmask_ref, dq_ref, dk_ref, dc_ref, o_ref, sfin_ref, s_scr):
    c = pl.program_id(1)

    @pl.when(c == 0)
    def _():
        s_scr[...] = jnp.zeros_like(s_scr)

    pb = pb_ref[...]
    cos, sin = cos_ref[...], sin_ref[...]
    q = _rotate(pb[:, 0:256], cos, sin, 32, 256)
    k = _rotate(pb[:, 256:512], cos, sin, 32, 256) * (HEAD_DIM ** -0.5)
    v = pb[:, 512:768]
    qd = q * dq_ref[...]
    kd = k * dk_ref[...]
    dc = dc_ref[...]
    outs = []
    for h in range(N_HEADS):
        sl = slice(h * HEAD_DIM, (h + 1) * HEAD_DIM)
        qk = _bdot_nt(q[:, sl], k[:, sl]) * dmask_ref[h]
        s = s_scr[h]
        outs.append(_bdot(qk, v[:, sl]) + _bdot(qd[:, sl], s))
        s_scr[h] = s * dc[h] + _bdot_tn(kd[:, sl], v[:, sl])
    o_ref[...] = _ret_norm_gate(outs, pb[:, 768:1024])

    @pl.when(c == pl.num_programs(1) - 1)
    def _():
        sfin_ref[...] = s_scr[...]


def _ret_prompt(p, nseq, seq_len):
    lg = _ret_log_decay()
    t = jnp.arange(RET_CHUNK, dtype=F32)
    diff = t[:, None] - t[None, :]
    dmask = jnp.where(diff >= 0, jnp.exp(lg[:, None, None] * jnp.maximum(diff, 0.0)), 0.0)
    lanes = lambda a: jnp.repeat(a, HEAD_DIM, axis=-1)
    dq = lanes(jnp.exp(lg[None, :] * (t[:, None] + 1.0)))
    dk = lanes(jnp.exp(lg[None, :] * (RET_CHUNK - 1.0 - t[:, None])))
    dc = jnp.broadcast_to(jnp.exp(lg * RET_CHUNK)[:, None, None], (N_HEADS, 1, HEAD_DIM))
    cos, sin = _rope_tables(jnp.arange(seq_len), 32, N_HEADS)
    nc = seq_len // RET_CHUNK
    full = lambda *s: pl.BlockSpec(s, lambda n, c: (0,) * len(s))
    return pl.pallas_call(
        _ret_chunk_kernel,
        grid=(nseq, nc),
        in_specs=[pl.BlockSpec((RET_CHUNK, 1024), lambda n, c: (n * nc + c, P_RET // 1024)),
                  pl.BlockSpec((RET_CHUNK, 256), lambda n, c: (c, 0)),
                  pl.BlockSpec((RET_CHUNK, 256), lambda n, c: (c, 0)),
                  full(N_HEADS, RET_CHUNK, RET_CHUNK), full(RET_CHUNK, 256), full(RET_CHUNK, 256),
                  full(N_HEADS, 1, HEAD_DIM)],
        out_specs=[pl.BlockSpec((RET_CHUNK, 256), lambda n, c: (n * nc + c, 0)),
                   pl.BlockSpec((None, N_HEADS, HEAD_DIM, HEAD_DIM), lambda n, c: (n, 0, 0, 0))],
        out_shape=[jax.ShapeDtypeStruct((nseq * seq_len, 256), F32),
                   jax.ShapeDtypeStruct((nseq, N_HEADS, HEAD_DIM, HEAD_DIM), F32)],
        scratch_shapes=[pltpu.VMEM((N_HEADS, HEAD_DIM, HEAD_DIM), F32)],
        compiler_params=_cparams(("parallel", "arbitrary")),
        name="retention_chunks",
    )(p, cos, sin, dmask, dq, dk, dc)


def _ret_single_kernel(pb_ref, cos_ref, sin_ref, dec_ref, s_ref, o_ref, so_ref, *, bs):
    pb = pb_ref[...]
    cos, sin = cos_ref[...], sin_ref[...]
    q = _rotate(pb[:, 0:256], cos, sin, 32, 256)
    k = _rotate(pb[:, 256:512], cos, sin, 32, 256) * (HEAD_DIM ** -0.5)
    v = pb[:, 512:768]
    eye = _eye64()
    dec = dec_ref[...]
    rows = []
    for n in range(bs):
        outs = []
        for h in range(N_HEADS):
            sl = slice(h * HEAD_DIM, (h + 1) * HEAD_DIM)
            qr, kr, vr = q[n:n + 1, sl], k[n:n + 1, sl], v[n:n + 1, sl]
            q_col = jnp.sum(jnp.where(eye, qr, 0.0), axis=-1, keepdims=True)
            k_col = jnp.sum(jnp.where(eye, kr, 0.0), axis=-1, keepdims=True)
            s = s_ref[n, h]
            d = dec[h]
            o = jnp.sum(qr * kr, axis=-1, keepdims=True) * vr + jnp.sum((q_col * d) * s, axis=0, keepdims=True)
            so_ref[n, h] = s * d + k_col * vr
            outs.append(o)
        rows.append(jnp.concatenate(outs, axis=-1))
    o = jnp.concatenate(rows, axis=0)
    g = pb[:, 768:1024]
    o_ref[...] = _ret_norm_gate([o[:, h * HEAD_DIM:(h + 1) * HEAD_DIM] for h in range(N_HEADS)], g)


def _ret_single(p, s0, pos, bs=8):
    nb = s0.shape[0]
    cos, sin = _rope_tables(jnp.full((1,), pos), 32, N_HEADS)
    dec = jnp.broadcast_to(jnp.exp(_ret_log_decay())[:, None, None], (N_HEADS, 1, HEAD_DIM))
    s_spec = pl.BlockSpec((bs, N_HEADS, HEAD_DIM, HEAD_DIM), lambda i: (i, 0, 0, 0))
    return pl.pallas_call(
        functools.partial(_ret_single_kernel, bs=bs),
        grid=(nb // bs,),
        in_specs=[pl.BlockSpec((bs, 1024), lambda i: (i, P_RET // 1024)),
                  pl.BlockSpec((1, 256), lambda i: (0, 0)), pl.BlockSpec((1, 256), lambda i: (0, 0)),
                  pl.BlockSpec((N_HEADS, 1, HEAD_DIM), lambda i: (0, 0, 0)), s_spec],
        out_specs=[pl.BlockSpec((bs, 256), lambda i: (i, 0)), s_spec],
        out_shape=[jax.ShapeDtypeStruct((nb, 256), F32), jax.ShapeDtypeStruct(s0.shape, F32)],
        compiler_params=_cparams(("parallel",)),
        name="retention_single",
    )(p, cos, sin, dec, s0)


def _mla_prep_kernel(cq_ref, ckv_ref, kr_ref, cos_ref, sin_ref, qng_ref, kvg_ref, wn_ref, wr_ref, wuk_ref,
                     qf_ref, rows_ref):
    cqn = _rms(cq_ref[...], qng_ref[...], n=Q_LORA)
    q_nope = _bdot(cqn, wn_ref[...]) * MLA_SCALE
    q_rope = _bdot(cqn, wr_ref[...]) * MLA_SCALE
    q_lat = _bdot(q_nope, wuk_ref[...])
    cos, sin = cos_ref[...], sin_ref[...]
    q_rope = _rotate(q_rope, cos, sin, 16, 128)
    kr = _rotate(kr_ref[...], cos, sin, 16, 128)
    lane = lax.broadcasted_iota(jnp.int32, kr.shape, 1)
    keep = lane < ROPE_DIM
    rows_ref[:, 0:KV_LORA] = _rms(ckv_ref[...], kvg_ref[...])
    rows_ref[:, KV_LORA:MLA_WP] = jnp.where(keep, kr, 0.0)
    for h in range(N_HEADS):
        qf_ref[h, :, 0:KV_LORA] = q_lat[:, h * KV_LORA:(h + 1) * KV_LORA]
        shifted = q_rope if h == 0 else pltpu.roll(q_rope, 128 - h * ROPE_DIM, 1)
        qf_ref[h, :, KV_LORA:MLA_WP] = jnp.where(keep, shifted, 0.0)


def _mla_prep(p, lw, pos, tm, tiles_per_seq):
    rows = p.shape[0]
    cos, sin = _rope_tables(pos, 16, 4)
    if pos.shape[0] == 1:
        tab = pl.BlockSpec((1, 128), lambda i: (0, 0))
    else:
        tab = pl.BlockSpec((tm, 128), lambda i: (i % tiles_per_seq, 0))
    full = lambda a, b: pl.BlockSpec((a, b), lambda i: (0, 0))
    return pl.pallas_call(
        _mla_prep_kernel,
        grid=(rows // tm,),
        in_specs=[pl.BlockSpec((tm, 256), lambda i: (i, P_CQ // 256)),
                  pl.BlockSpec((tm, 256), lambda i: (i, P_CKV // 256)),
                  pl.BlockSpec((tm, 128), lambda i: (i, P_KRFL // 128)),
                  tab, tab, full(1, 256), full(1, 256), full(256, 256), full(256, 128), full(256, 1024)],
        out_specs=[pl.BlockSpec((N_HEADS, tm, MLA_WP), lambda i: (0, i, 0)),
                   pl.BlockSpec((tm, MLA_WP), lambda i: (i, 0))],
        out_shape=[jax.ShapeDtypeStruct((N_HEADS, rows, MLA_WP), F32),
                   jax.ShapeDtypeStruct((rows, MLA_WP), F32)],
        compiler_params=_cparams(("parallel",)),
        name="mla_prep",
    )(p, p, p, cos, sin, lw["qn_g"], lw["kvn_g"], lw["w_nope"], lw["w_rope"], lw["wuk_bd"])


def _tri_schedule(n):
    qi = np.concatenate([np.full((i + 1,), i, np.int32) for i in range(n)])
    kj = np.concatenate([np.arange(i + 1, dtype=np.int32) for i in range(n)])
    return jnp.asarray(qi), jnp.asarray(kj)


def _online_softmax(s, m_ref, l_ref):
    m_prev = m_ref[...]
    m_new = jnp.maximum(m_prev, jnp.max(s, axis=-1, keepdims=True))
    alpha = jnp.exp(m_prev - m_new)
    p = jnp.exp(s - m_new)
    l_ref[...] = alpha * l_ref[...] + jnp.sum(p, axis=-1, keepdims=True)
    m_ref[...] = m_new
    return p, alpha


def _mla_flash_kernel(qi_ref, kj_ref, q_ref, kv_ref, wuv_ref, o_ref, m_scr, l_scr, acc_scr, *, tq):
    step = pl.program_id(1)
    i, j = qi_ref[step], kj_ref[step]

    @pl.when(j == 0)
    def _():
        m_scr[...] = jnp.full_like(m_scr, NEG)
        l_scr[...] = jnp.zeros_like(l_scr)
        acc_scr[...] = jnp.zeros_like(acc_scr)

    q = q_ref[...].reshape(N_HEADS * tq, MLA_WP)
    kv = kv_ref[...].astype(BF16)
    s = _bdot_nt(q, kv)
    row = lax.broadcasted_iota(jnp.int32, s.shape, 0) % tq
    col = lax.broadcasted_iota(jnp.int32, s.shape, 1)
    s = jnp.where((j < i) | (col <= row), s, NEG)
    p, alpha = _online_softmax(s, m_scr, l_scr)
    acc_scr[...] = alpha * acc_scr[...] + jnp.dot(p.astype(BF16), kv[:, 0:KV_LORA], preferred_element_type=F32)

    @pl.when(j == i)
    def _():
        o = acc_scr[...] / l_scr[...]
        o_ref[...] = jnp.concatenate(
            [_bdot(o[h * tq:(h + 1) * tq], wuv_ref[h]) for h in range(N_HEADS)], axis=-1)


def _mla_flash(qf, rows, wuv, nseq, seq_len, tq):
    nq = seq_len // tq
    qi, kj = _tri_schedule(nq)
    grid_spec = pltpu.PrefetchScalarGridSpec(
        num_scalar_prefetch=2,
        grid=(nseq, int(qi.shape[0])),
        in_specs=[pl.BlockSpec((N_HEADS, tq, MLA_WP), lambda n, s, qi, kj: (0, n * nq + qi[s], 0)),
                  pl.BlockSpec((tq, MLA_WP), lambda n, s, qi, kj: (n * nq + kj[s], 0)),
                  pl.BlockSpec((N_HEADS, KV_LORA, HEAD_DIM), lambda n, s, qi, kj: (0, 0, 0))],
        out_specs=pl.BlockSpec((tq, 256), lambda n, s, qi, kj: (n * nq + qi[s], 0)),
        scratch_shapes=[pltpu.VMEM((N_HEADS * tq, 1), F32), pltpu.VMEM((N_HEADS * tq, 1), F32),
                        pltpu.VMEM((N_HEADS * tq, KV_LORA), F32)])
    return pl.pallas_call(
        functools.partial(_mla_flash_kernel, tq=tq),
        grid_spec=grid_spec,
        out_shape=jax.ShapeDtypeStruct((nseq * seq_len, 256), F32),
        compiler_params=_cparams(("parallel", "arbitrary")),
        name="mla_prompt_attention",
    )(qi, kj, qf, rows, wuv)


def _mla_decode_kernel(pt_ref, q_ref, new_ref, wuv_ref, c_hbm, o_ref, kbuf, sem, m_scr, l_scr, acc_scr,
                       *, layer, pg, n_pages):
    b = pl.program_id(0)
    n_groups = n_pages // pg

    def page_copies(seq, grp, slot):
        return [pltpu.make_async_copy(c_hbm.at[layer, pt_ref[seq, grp * pg + i]], kbuf.at[slot, i], sem.at[slot])
                for i in range(pg)]

    @pl.when(b == 0)
    def _():
        for cp in page_copies(0, 0, 0):
            cp.start()

    m_scr[...] = jnp.full_like(m_scr, NEG)
    l_scr[...] = jnp.zeros_like(l_scr)
    acc_scr[...] = jnp.zeros_like(acc_scr)

    def group(grp, carry):
        slot = (b * n_groups + grp) % 2

        @pl.when(grp + 1 < n_groups)
        def _():
            for cp in page_copies(b, grp + 1, 1 - slot):
                cp.start()

        @pl.when(jnp.logical_and(grp + 1 == n_groups, b + 1 < pl.num_programs(0)))
        def _():
            for cp in page_copies(b + 1, 0, 1 - slot):
                cp.start()

        for cp in page_copies(b, grp, slot):
            cp.wait()

        q = q_ref[...].astype(BF16)
        q_lat, q_rope = q[:, 0:KV_LORA], q[:, KV_LORA:MLA_W]
        kts = [kbuf[slot, i].astype(BF16) for i in range(pg)]
        s = jnp.concatenate(
            [jnp.dot(q_lat, kt[0:KV_LORA], preferred_element_type=F32)
             + jnp.dot(q_rope, kt[KV_LORA:MLA_W], preferred_element_type=F32)
             for kt in kts], axis=-1)
        p, alpha = _online_softmax(s, m_scr, l_scr)
        p = p.astype(BF16)
        acc = alpha * acc_scr[...]
        for i, kt in enumerate(kts):
            acc = acc + lax.dot_general(p[:, i * PAGE:(i + 1) * PAGE], kt[0:KV_LORA], _NT,
                                        preferred_element_type=F32)
        acc_scr[...] = acc
        return carry

    lax.fori_loop(0, n_groups, group, 0)

    new = new_ref[...]
    s_new = jnp.sum(q_ref[...] * new, axis=-1, keepdims=True)
    m_prev = m_scr[...]
    m_new = jnp.maximum(m_prev, s_new)
    a = jnp.exp(m_prev - m_new)
    p_new = jnp.exp(s_new - m_new)
    l = a * l_scr[...] + p_new
    o = (a * acc_scr[...] + p_new * new[:, 0:KV_LORA]) / l
    o_ref[...] = jnp.concatenate([_bdot(o[h:h + 1], wuv_ref[h]) for h in range(N_HEADS)], axis=-1)


def _mla_decode(cache, layer, page_table, q, new_rows, wuv, pg):
    nb, n_pages = page_table.shape
    grid_spec = pltpu.PrefetchScalarGridSpec(
        num_scalar_prefetch=1,
        grid=(nb,),
        in_specs=[pl.BlockSpec((None, 8, MLA_WP), lambda b, pt: (b, 0, 0)),
                  pl.BlockSpec((None, 1, MLA_WP), lambda b, pt: (b, 0, 0)),
                  pl.BlockSpec((N_HEADS, KV_LORA, HEAD_DIM), lambda b, pt: (0, 0, 0)),
                  pl.BlockSpec(memory_space=pl.ANY)],
        out_specs=pl.BlockSpec((None, 1, 256), lambda b, pt: (b, 0, 0)),
        scratch_shapes=[pltpu.VMEM((2, pg, MLA_W, PAGE), F32), pltpu.SemaphoreType.DMA((2,)),
                        pltpu.VMEM((8, 1), F32), pltpu.VMEM((8, 1), F32), pltpu.VMEM((8, KV_LORA), F32)])
    out = pl.pallas_call(
        functools.partial(_mla_decode_kernel, layer=layer, pg=pg, n_pages=n_pages),
        grid_spec=grid_spec,
        out_shape=jax.ShapeDtypeStruct((nb, 1, 256), F32),
        compiler_params=_cparams(("arbitrary",)),
        name="mla_paged_attention",
    )(page_table, q, new_rows, wuv, cache)
    return out.reshape(nb, 256)


def _log_sigmoid(x):
    return jnp.minimum(x, 0.0) - jnp.log(1.0 + jnp.exp(-jnp.abs(x)))


def _fox_prep_kernel(fq_ref, fk_ref, fv_ref, fl_ref, bf_ref, tri_ref, q_o, k_o, v_o, lf_o, c_o, ct_o, carry,
                     *, cumulative):
    fq = fq_ref[...] * FOX_SCALE
    fk, fv = fk_ref[...], fv_ref[...]
    for h in range(N_HEADS):
        q_o[h] = fq[:, h * HEAD_DIM:(h + 1) * HEAD_DIM]
    for g in range(2):
        k_o[g] = fk[:, g * HEAD_DIM:(g + 1) * HEAD_DIM]
        v_o[g] = fv[:, g * HEAD_DIM:(g + 1) * HEAD_DIM]
    x = pltpu.roll(fl_ref[...] + bf_ref[...], 128 - ROPE_DIM, 1)
    lane = lax.broadcasted_iota(jnp.int32, x.shape, 1)
    lf = jnp.where(lane < N_HEADS, _log_sigmoid(x), 0.0)
    lf_o[...] = lf
    if cumulative:
        @pl.when(pl.program_id(1) == 0)
        def _():
            carry[...] = jnp.zeros_like(carry)
        c = _dot01_left(tri_ref[...], lf) + carry[...]
        carry[...] = c[c.shape[0] - 1:c.shape[0], :]
    else:
        c = lf
    c_o[...] = c
    ct_o[...] = jnp.transpose(c)[0:8, :]


def _fox_prep(p, bf, tm, nseq, seq_len, cumulative):
    rows = p.shape[0]
    tps = seq_len // tm if cumulative else rows // tm
    gn = nseq if cumulative else 1
    tri = (jnp.arange(tm)[:, None] >= jnp.arange(tm)[None, :]).astype(BF16)
    idx = lambda w, off: (lambda n, i: (n * tps + i, off // w))
    out_idx = lambda n, i: (n * tps + i, 0)
    hm = lambda k: pl.BlockSpec((k, tm, HEAD_DIM), lambda n, i: (0, n * tps + i, 0))
    return pl.pallas_call(
        functools.partial(_fox_prep_kernel, cumulative=cumulative),
        grid=(gn, tps),
        in_specs=[pl.BlockSpec((tm, 256), idx(256, P_FQ)), pl.BlockSpec((tm, 128), idx(128, P_FK)),
                  pl.BlockSpec((tm, 128), idx(128, P_FV)), pl.BlockSpec((tm, 128), idx(128, P_KRFL)),
                  pl.BlockSpec((1, 128), lambda n, i: (0, 0)), pl.BlockSpec((tm, tm), lambda n, i: (0, 0))],
        out_specs=[hm(4), hm(2), hm(2), pl.BlockSpec((tm, 128), out_idx), pl.BlockSpec((tm, 128), out_idx),
                   pl.BlockSpec((8, tm), lambda n, i: (0, n * tps + i))],
        out_shape=[jax.ShapeDtypeStruct((4, rows, HEAD_DIM), F32), jax.ShapeDtypeStruct((2, rows, HEAD_DIM), F32),
                   jax.ShapeDtypeStruct((2, rows, HEAD_DIM), F32), jax.ShapeDtypeStruct((rows, 128), F32),
                   jax.ShapeDtypeStruct((rows, 128), F32), jax.ShapeDtypeStruct((8, rows), F32)],
        scratch_shapes=[pltpu.VMEM((1, 128), F32)],
        compiler_params=_cparams(("parallel", "arbitrary")),
        name="fox_prep",
    )(p, p, p, p, bf, tri)


def _fox_flash_kernel(qi_ref, kj_ref, q_ref, k_ref, v_ref, c_ref, ct_ref, o_ref, m_scr, l_scr, acc_scr, *, tq):
    step = pl.program_id(1)
    i, j = qi_ref[step], kj_ref[step]

    @pl.when(j == 0)
    def _():
        m_scr[...] = jnp.full_like(m_scr, NEG)
        l_scr[...] = jnp.zeros_like(l_scr)
        acc_scr[...] = jnp.zeros_like(acc_scr)

    row = lax.broadcasted_iota(jnp.int32, (tq, tq), 0)
    col = lax.broadcasted_iota(jnp.int32, (tq, tq), 1)
    visible = (j < i) | (col <= row)
    c = c_ref[...]
    for h in range(N_HEADS):
        g = h // 2
        s = _bdot_nt(q_ref[h], k_ref[g]) + (c[:, h:h + 1] - ct_ref[h:h + 1, :])
        s = jnp.where(visible, s, NEG)
        p, alpha = _online_softmax(s, m_scr.at[h], l_scr.at[h])
        acc_scr[h] = alpha * acc_scr[h] + _bdot(p, v_ref[g])

    @pl.when(j == i)
    def _():
        o_ref[...] = jnp.concatenate([acc_scr[h] / l_scr[h] for h in range(N_HEADS)], axis=-1)


def _fox_flash(q, k, v, c, ct, nseq, seq_len, tq):
    nq = seq_len // tq
    qi, kj = _tri_schedule(nq)
    qmap = lambda n, s, qi, kj: (0, n * nq + qi[s], 0)
    kmap = lambda n, s, qi, kj: (0, n * nq + kj[s], 0)
    grid_spec = pltpu.PrefetchScalarGridSpec(
        num_scalar_prefetch=2,
        grid=(nseq, int(qi.shape[0])),
        in_specs=[pl.BlockSpec((4, tq, HEAD_DIM), qmap), pl.BlockSpec((2, tq, HEAD_DIM), kmap),
                  pl.BlockSpec((2, tq, HEAD_DIM), kmap),
                  pl.BlockSpec((tq, 128), lambda n, s, qi, kj: (n * nq + qi[s], 0)),
                  pl.BlockSpec((8, tq), lambda n, s, qi, kj: (0, n * nq + kj[s]))],
        out_specs=pl.BlockSpec((tq, 256), lambda n, s, qi, kj: (n * nq + qi[s], 0)),
        scratch_shapes=[pltpu.VMEM((4, tq, 1), F32), pltpu.VMEM((4, tq, 1), F32),
                        pltpu.VMEM((4, tq, HEAD_DIM), F32)])
    return pl.pallas_call(
        functools.partial(_fox_flash_kernel, tq=tq),
        grid_spec=grid_spec,
        out_shape=jax.ShapeDtypeStruct((nseq * seq_len, 256), F32),
        compiler_params=_cparams(("parallel", "arbitrary")),
        name="fox_prompt_attention",
    )(qi, kj, q, k, v, c, ct)


def _fox_decode_kernel(pt_ref, q_ref, kn_ref, vn_ref, cn_ref, slt_ref, k_hbm, v_hbm, lf_hbm, o_ref,
                       kbuf, vbuf, lfbuf, sem, m_scr, l_scr, acc_scr, run_scr, lf_scr, *, layer, pg, n_pages):
    b = pl.program_id(0)
    n_groups = n_pages // pg

    def page_copies(seq, grp, slot):
        out = []
        for i in range(pg):
            page = pt_ref[seq, n_pages - 1 - (grp * pg + i)]
            out.append(pltpu.make_async_copy(k_hbm.at[layer, page], kbuf.at[slot, i], sem.at[slot, 0]))
            out.append(pltpu.make_async_copy(v_hbm.at[layer, page], vbuf.at[slot, i], sem.at[slot, 1]))
            out.append(pltpu.make_async_copy(lf_hbm.at[layer, page], lfbuf.at[slot, i], sem.at[slot, 2]))
        return out

    @pl.when(b == 0)
    def _():
        for cp in page_copies(0, 0, 0):
            cp.start()
        lf_scr[...] = jnp.zeros_like(lf_scr)

    m_scr[...] = jnp.full_like(m_scr, NEG)
    l_scr[...] = jnp.zeros_like(l_scr)
    acc_scr[...] = jnp.zeros_like(acc_scr)
    run_scr[...] = jnp.zeros_like(run_scr)

    def group(grp, carry):
        slot = (b * n_groups + grp) % 2

        @pl.when(grp + 1 < n_groups)
        def _():
            for cp in page_copies(b, grp + 1, 1 - slot):
                cp.start()

        @pl.when(jnp.logical_and(grp + 1 == n_groups, b + 1 < pl.num_programs(0)))
        def _():
            for cp in page_copies(b + 1, 0, 1 - slot):
                cp.start()

        for cp in page_copies(b, grp, slot):
            cp.wait()

        for i in range(pg):
            lf_scr[i * 8:i * 8 + N_HEADS, :] = lfbuf[slot, i]
        lf = lf_scr[...]
        within = _dot01(lf, slt_ref[...])
        totals = jnp.sum(lf, axis=-1, keepdims=True)
        q = q_ref[...].astype(BF16)
        run = run_scr[...] + cn_ref[...]
        tiles = []
        for i in range(pg):
            kt = kbuf[slot, i].reshape(2 * HEAD_DIM, PAGE).astype(BF16)
            s = jnp.dot(q, kt, preferred_element_type=F32)
            tiles.append(s + within[i * 8:(i + 1) * 8, :] + run)
            run = run + totals[i * 8:(i + 1) * 8, :]
        run_scr[...] = run - cn_ref[...]
        s = jnp.concatenate(tiles, axis=-1)
        p, alpha = _online_softmax(s, m_scr, l_scr)
        p = p.astype(BF16)
        acc = alpha * acc_scr[...]
        for i in range(pg):
            vt = vbuf[slot, i].reshape(2 * HEAD_DIM, PAGE).astype(BF16)
            acc = acc + lax.dot_general(p[:, i * PAGE:(i + 1) * PAGE], vt, _NT, preferred_element_type=F32)
        acc_scr[...] = acc
        return carry

    lax.fori_loop(0, n_groups, group, 0)

    s_new = jnp.sum(q_ref[...] * kn_ref[...], axis=-1, keepdims=True)
    m_prev = m_scr[...]
    m_new = jnp.maximum(m_prev, s_new)
    a = jnp.exp(m_prev - m_new)
    p_new = jnp.exp(s_new - m_new)
    l = a * l_scr[...] + p_new
    o = (a * acc_scr[...] + p_new * vn_ref[...]) / l
    o_ref[...] = jnp.concatenate(
        [o[h:h + 1, (h // 2) * HEAD_DIM:(h // 2 + 1) * HEAD_DIM] for h in range(N_HEADS)], axis=-1)


def _fox_decode(cache_k, cache_v, cache_lf, layer, page_table, q8, k_new, v_new, cn8, pg):
    nb, n_pages = page_table.shape
    slt = (jnp.arange(PAGE)[:, None] > jnp.arange(PAGE)[None, :]).astype(BF16)
    per_seq = lambda r, w: pl.BlockSpec((None, r, w), lambda b, pt: (b, 0, 0))
    hbm = pl.BlockSpec(memory_space=pl.ANY)
    grid_spec = pltpu.PrefetchScalarGridSpec(
        num_scalar_prefetch=1,
        grid=(nb,),
        in_specs=[per_seq(8, 128), per_seq(1, 128), per_seq(1, 128), per_seq(8, 1),
                  pl.BlockSpec((PAGE, PAGE), lambda b, pt: (0, 0)), hbm, hbm, hbm],
        out_specs=pl.BlockSpec((None, 1, 256), lambda b, pt: (b, 0, 0)),
        scratch_shapes=[pltpu.VMEM((2, pg, 2, HEAD_DIM, PAGE), F32), pltpu.VMEM((2, pg, 2, HEAD_DIM, PAGE), F32),
                        pltpu.VMEM((2, pg, N_HEADS, PAGE), F32), pltpu.SemaphoreType.DMA((2, 3)),
                        pltpu.VMEM((8, 1), F32), pltpu.VMEM((8, 1), F32), pltpu.VMEM((8, 128), F32),
                        pltpu.VMEM((8, 1), F32), pltpu.VMEM((pg * 8, PAGE), F32)])
    out = pl.pallas_call(
        functools.partial(_fox_decode_kernel, layer=layer, pg=pg, n_pages=n_pages),
        grid_spec=grid_spec,
        out_shape=jax.ShapeDtypeStruct((nb, 1, 256), F32),
        compiler_params=_cparams(("arbitrary",)),
        name="fox_paged_attention",
    )(page_table, q8, k_new, v_new, cn8, slt, cache_k, cache_v, cache_lf)
    return out.reshape(nb, 256)


def _merge_kernel(x_ref, pg_ref, ya_ref, yb_ref, yc_ref, yd_ref, wb_ref, wo_ref, o_ref):
    pg = pg_ref[...]
    merged = None
    for bi, y_ref in enumerate((ya_ref, yb_ref, yc_ref, yd_ref)):
        gate = _sigmoid(pg[:, bi * D_MODEL:(bi + 1) * D_MODEL])
        term = gate * jnp.dot(y_ref[...].astype(BF16), wb_ref[bi], preferred_element_type=F32)
        merged = term if merged is None else merged + term
    o_ref[...] = x_ref[...] + jnp.dot(merged.astype(BF16), wo_ref[...], preferred_element_type=F32)


def _merge(x, p, ya, yb, yc, yd, wb, wo, tm):
    rows = x.shape[0]
    row = lambda w: pl.BlockSpec((tm, w), lambda i: (i, 0))
    return pl.pallas_call(
        _merge_kernel,
        grid=(rows // tm,),
        in_specs=[row(D_MODEL), pl.BlockSpec((tm, 4 * D_MODEL), lambda i: (i, P_GATE)),
                  row(256), row(256), row(256), row(256),
                  pl.BlockSpec((4, BRANCH_W, D_MODEL), lambda i: (0, 0, 0)),
                  pl.BlockSpec((D_MODEL, D_MODEL), lambda i: (0, 0))],
        out_specs=row(D_MODEL),
        out_shape=jax.ShapeDtypeStruct((rows, D_MODEL), F32),
        compiler_params=_cparams(("parallel",)),
        name="gated_merge",
    )(x, p, ya, yb, yc, yd, wb, wo)


def _mlp_kernel(x_ref, g_ref, wu_ref, wd_ref, fg_ref, *o_refs, final):
    x = x_ref[...]
    h = _rms(x, g_ref[...]).astype(BF16)
    u = jnp.maximum(jnp.dot(h, wu_ref[...], preferred_element_type=F32), 0.0)
    y = x + jnp.dot((u * u).astype(BF16), wd_ref[...], preferred_element_type=F32)
    o_refs[0][...] = y
    if final:
        o_refs[1][...] = _rms(y, fg_ref[...])


def _mlp(x, g, wu, wd, fg, tm, final):
    rows = x.shape[0]
    row = pl.BlockSpec((tm, D_MODEL), lambda i: (i, 0))
    vec = pl.BlockSpec((1, D_MODEL), lambda i: (0, 0))
    const = lambda a, b: pl.BlockSpec((a, b), lambda i: (0, 0), pipeline_mode=pl.Buffered(1))
    n_out = 2 if final else 1
    return pl.pallas_call(
        functools.partial(_mlp_kernel, final=final),
        grid=(rows // tm,),
        in_specs=[row, vec, const(D_MODEL, D_FF), const(D_FF, D_MODEL), vec],
        out_specs=[row] * n_out,
        out_shape=[jax.ShapeDtypeStruct((rows, D_MODEL), F32)] * n_out,
        compiler_params=_cparams(("parallel",)),
        name="mlp",
    )(x, g, wu, wd, fg)


def _layer(x, lw, consts, *, nseq, seq_len, tm, decode):
    rows = x.shape[0]
    p = _norm_matmul(x, lw["norm1_g"], lw["w_in"], tm)
    prompt = decode is None

    if prompt:
        r, w, kh, v, kk, b, g = _rwkv_prep(p, None, lw, consts["bd"], tm, True, seq_len)
        y, rwkv_new = _rwkv_scan(r, w, kh, v, kk, b, nseq, seq_len)
    else:
        r, w, kh, v, kk, b, g = _rwkv_prep(p, decode["shift"], lw, consts["bd"], tm, False, seq_len)
        y, rwkv_new = _rwkv_single(r, w, kh, v, kk, b, decode["rwkv"])
    ya = _rwkv_post(y, r, kh, v, g, lw, tm)
    shift_new = p[:, P_RWKV:P_RWKV + RWKV_IN].reshape(nseq, seq_len, RWKV_IN)[:, -1]

    if prompt:
        yb, ret_new = _ret_prompt(p, nseq, seq_len)
    else:
        yb, ret_new = _ret_single(p, decode["ret"], decode["pos"])

    pos = jnp.arange(seq_len) if prompt else jnp.full((1,), decode["pos"])
    qf, mla_rows = _mla_prep(p, lw, pos, tm, max(seq_len // tm, 1))
    if prompt:
        yc = _mla_flash(qf, mla_rows, lw["wuv"], nseq, seq_len, min(ATTN_TILE, seq_len))
    else:
        q8 = jnp.pad(jnp.transpose(qf, (1, 0, 2)), ((0, 0), (0, 8 - N_HEADS), (0, 0)))
        yc = _mla_decode(decode["cache_mla"], decode["layer"], decode["page_table"], q8,
                         mla_rows.reshape(rows, 1, MLA_WP), lw["wuv"], decode["pg_mla"])

    fq, fk, fv, lf, c, ct = _fox_prep(p, lw["bf"], tm, nseq, seq_len, prompt)
    if prompt:
        yd = _fox_flash(fq, fk, fv, c, ct, nseq, seq_len, min(ATTN_TILE, seq_len))
    else:
        q8 = jnp.zeros((rows, 8, 128), F32)
        for h in range(N_HEADS):
            gq = (h // 2) * HEAD_DIM
            q8 = q8.at[:, h, gq:gq + HEAD_DIM].set(fq[h])
        cn8 = jnp.pad(lf[:, 0:N_HEADS], ((0, 0), (0, 8 - N_HEADS))).reshape(rows, 8, 1)
        k_new = p[:, P_FK:P_FK + 128].reshape(rows, 1, 128)
        v_new = p[:, P_FV:P_FV + 128].reshape(rows, 1, 128)
        yd = _fox_decode(decode["cache_k"], decode["cache_v"], decode["cache_lf"], decode["layer"],
                         decode["page_table"], q8, k_new, v_new, cn8, decode["pg_fox"])

    x1 = _merge(x, p, ya, yb, yc, yd, lw["w_branch"], lw["w_out"], tm)
    new = (mla_rows[:, 0:MLA_W], p[:, P_FK:P_FK + 128], p[:, P_FV:P_FV + 128], lf[:, 0:N_HEADS],
           rwkv_new, shift_new, ret_new)
    return x1, new


def _layer_weights(l, norm1_g, norm2_g, w_in, rwkv_mu, rwkv_w0, rwkv_w2, rwkv_a0, rwkv_a2, rwkv_g2, rwkv_kk,
                   rwkv_ka, rwkv_rk, rwkv_ln_g, rwkv_ln_b, mla_qn_g, mla_kvn_g, mla_wuq, mla_wuk, mla_wuv,
                   fox_bf, w_branch, w_out, w_up, w_down):
    wi = w_in[l]
    a, b = wi[:, 0:1024], wi[:, 1024:2048]
    cq, ckv, kr = wi[:, 2048:2240], wi[:, 2240:2496], wi[:, 2496:2528]
    fq, fk, fv, fl = wi[:, 2528:2784], wi[:, 2784:2912], wi[:, 2912:3040], wi[:, 3040:3044]
    gate = wi[:, 3044:7140]
    z = lambda n: jnp.zeros((D_MODEL, n), F32)
    w_all = jnp.concatenate([gate, a, b, fq, fk, fv, cq, z(256 - Q_LORA), ckv, kr, fl, z(128 - ROPE_DIM - 4)],
                            axis=1).astype(BF16)
    row = lambda v: v.reshape(1, -1)
    wuq = jnp.pad(mla_wuq[l], ((0, 256 - Q_LORA), (0, 0), (0, 0)))
    wuk_bd = jnp.zeros((N_HEADS * NOPE_DIM, N_HEADS * KV_LORA), F32)
    for h in range(N_HEADS):
        wuk_bd = wuk_bd.at[h * NOPE_DIM:(h + 1) * NOPE_DIM, h * KV_LORA:(h + 1) * KV_LORA].set(mla_wuk[l][:, h, :].T)
    return dict(
        norm1_g=row(norm1_g[l]), norm2_g=row(norm2_g[l]), w_in=w_all,
        mu=row(rwkv_mu[l]), w0=row(rwkv_w0[l]), w2=rwkv_w2[l].astype(BF16), a0=row(rwkv_a0[l]),
        a2=rwkv_a2[l].astype(BF16), g2=rwkv_g2[l].astype(BF16), kkp=row(rwkv_kk[l]), ka=row(rwkv_ka[l]),
        rk=rwkv_rk[l].reshape(N_HEADS, 1, HEAD_DIM), ln_g=row(rwkv_ln_g[l]), ln_b=row(rwkv_ln_b[l]),
        qn_g=row(jnp.pad(mla_qn_g[l], (0, 256 - Q_LORA))), kvn_g=row(mla_kvn_g[l]),
        w_nope=wuq[:, :, 0:NOPE_DIM].reshape(256, N_HEADS * NOPE_DIM).astype(BF16),
        w_rope=wuq[:, :, NOPE_DIM:].reshape(256, N_HEADS * ROPE_DIM).astype(BF16),
        wuk_bd=wuk_bd.astype(BF16),
        wuv=jnp.transpose(mla_wuv[l], (1, 0, 2)).astype(BF16),
        bf=jnp.pad(fox_bf[l], (ROPE_DIM, 128 - ROPE_DIM - 4)).reshape(1, 128),
        w_branch=w_branch[l].astype(BF16), w_out=w_out[l].astype(BF16),
        w_up=w_up[l].astype(BF16), w_down=w_down[l].astype(BF16))


def kernel(x_prompt, x_sample, cache_mla, cache_fox_k, cache_fox_v, cache_fox_logf, state_rwkv, state_rwkv_shift, state_ret, page_table, norm1_g, norm2_g, final_g, w_in, rwkv_mu, rwkv_w0, rwkv_w2, rwkv_a0, rwkv_a2, rwkv_g2, rwkv_kk, rwkv_ka, rwkv_rk, rwkv_ln_g, rwkv_ln_b, mla_qn_g, mla_kvn_g, mla_wuq, mla_wuk, mla_wuv, fox_bf, w_branch, w_out, w_up, w_down):
    nseq, seq_len = x_prompt.shape[:2]
    nb, dec_len = x_sample.shape[:2]
    assert dec_len == 1
    depth = w_in.shape[0]
    n_pages = page_table.shape[1]
    t_past = n_pages * PAGE
    n_pool = cache_mla.shape[1]
    pg_mla = min(32, n_pages)
    pg_fox = min(32, n_pages)
    tm_p = min(512, seq_len)
    tm_s = nb

    cache_mla_t = jnp.transpose(cache_mla, (0, 1, 3, 2))
    cache_k_t = jnp.transpose(cache_fox_k, (0, 1, 3, 4, 2))
    cache_v_t = jnp.transpose(cache_fox_v, (0, 1, 3, 4, 2))
    cache_lf_t = jnp.transpose(cache_fox_logf, (0, 1, 3, 2))
    consts = dict(bd=(jnp.arange(256)[:, None] // HEAD_DIM == jnp.arange(256)[None, :] // HEAD_DIM).astype(BF16))
    fg = final_g.reshape(1, D_MODEL)

    xp = x_prompt.reshape(nseq * seq_len, D_MODEL)
    xs = x_sample.reshape(nb, D_MODEL)
    new_p, new_s = [], []
    yp = ys = None
    for l in range(depth):
        lw = _layer_weights(l, norm1_g, norm2_g, w_in, rwkv_mu, rwkv_w0, rwkv_w2, rwkv_a0, rwkv_a2, rwkv_g2,
                            rwkv_kk, rwkv_ka, rwkv_rk, rwkv_ln_g, rwkv_ln_b, mla_qn_g, mla_kvn_g, mla_wuq,
                            mla_wuk, mla_wuv, fox_bf, w_branch, w_out, w_up, w_down)
        final = l == depth - 1
        x1, st = _layer(xp, lw, consts, nseq=nseq, seq_len=seq_len, tm=tm_p, decode=None)
        outs = _mlp(x1, lw["norm2_g"], lw["w_up"], lw["w_down"], fg, tm_p, final)
        xp = outs[0]
        if final:
            yp = outs[1]
        new_p.append(st)
        decode = dict(shift=state_rwkv_shift[l], rwkv=state_rwkv[l], ret=state_ret[l], pos=t_past, layer=l,
                      page_table=page_table, cache_mla=cache_mla_t, cache_k=cache_k_t, cache_v=cache_v_t,
                      cache_lf=cache_lf_t, pg_mla=pg_mla, pg_fox=pg_fox)
        x1, st = _layer(xs, lw, consts, nseq=nb, seq_len=1, tm=tm_s, decode=decode)
        outs = _mlp(x1, lw["norm2_g"], lw["w_up"], lw["w_down"], fg, tm_s, final)
        xs = outs[0]
        if final:
            ys = outs[1]
        new_s.append(st)

    def stack(new, i, shape):
        return jnp.stack([st[i] for st in new]).reshape((depth,) + shape)

    res = [yp.reshape(nseq, seq_len, D_MODEL), ys.reshape(nb, 1, D_MODEL)]
    shapes_p = [(nseq, seq_len, MLA_W), (nseq, seq_len, 2, HEAD_DIM), (nseq, seq_len, 2, HEAD_DIM),
                (nseq, seq_len, N_HEADS), (nseq, N_HEADS, HEAD_DIM, HEAD_DIM), (nseq, RWKV_IN),
                (nseq, N_HEADS, HEAD_DIM, HEAD_DIM)]
    shapes_s = [(nb, 1, MLA_W), (nb, 1, 2, HEAD_DIM), (nb, 1, 2, HEAD_DIM), (nb, 1, N_HEADS),
                (nb, N_HEADS, HEAD_DIM, HEAD_DIM), (nb, RWKV_IN), (nb, N_HEADS, HEAD_DIM, HEAD_DIM)]
    for i in range(7):
        res.append(stack(new_p, i, shapes_p[i]))
        res.append(stack(new_s, i, shapes_s[i]))
    return tuple(res)
```

```python
import functools

import jax
import jax.numpy as jnp
import numpy as np
from jax import lax
from jax.experimental import pallas as pl
from jax.experimental.pallas import tpu as pltpu

F32 = jnp.float32
BF16 = jnp.bfloat16

D_MODEL = 1024
HEAD_DIM = 64
N_HEADS = 4
BRANCH_W = 256
PAGE = 128
DECAY_LORA = 64
ICLR_LORA = 64
GATE_LORA = 128
RWKV_IN = 1024
DECAY_SCALE = 0.6065306597126334
GN_EPS = 64e-5
RET_CHUNK = 128
Q_LORA = 192
KV_LORA = 256
NOPE_DIM = 64
ROPE_DIM = 32
MLA_W = KV_LORA + ROPE_DIM
MLA_WP = 384
MLA_SCALE = (NOPE_DIM + ROPE_DIM) ** -0.5
FOX_SCALE = HEAD_DIM ** -0.5
D_FF = 4096
ROPE_BASE = 10000.0
NORM_EPS = 1e-6
NEG = -1e30

P_GATE = 0
P_RWKV = 4096
P_RET = 5120
P_FQ = 6144
P_FK = 6400
P_FV = 6528
P_CQ = 6656
P_CKV = 6912
P_KRFL = 7168
P_W = 7296
P_TN = 2432

VMEM_LIMIT = 56 * 1024 * 1024
ATTN_TILE = 512


def _cparams(sem):
    return pltpu.CompilerParams(dimension_semantics=sem, vmem_limit_bytes=VMEM_LIMIT)


def _bdot(a, b):
    return jnp.dot(a.astype(BF16), b.astype(BF16), preferred_element_type=F32)


def _bdot_nt(a, b):
    return lax.dot_general(a.astype(BF16), b.astype(BF16), (((1,), (1,)), ((), ())),
                           preferred_element_type=F32)


def _bdot_tn(a, b):
    return lax.dot_general(a.astype(BF16), b.astype(BF16), (((0,), (0,)), ((), ())),
                           preferred_element_type=F32)


def _split3(a):
    a1 = a.astype(BF16)
    r1 = a - a1.astype(F32)
    a2 = r1.astype(BF16)
    a3 = (r1 - a2.astype(F32)).astype(BF16)
    return a1, a2, a3


def _dot01(a, ones01):
    a1, a2, a3 = _split3(a)
    d = lambda p: jnp.dot(p, ones01, preferred_element_type=F32)
    return d(a1) + (d(a2) + d(a3))


def _dot01_left(ones01, a):
    a1, a2, a3 = _split3(a)
    d = lambda p: jnp.dot(ones01, p, preferred_element_type=F32)
    return d(a1) + (d(a2) + d(a3))


def _sigmoid(x):
    return 1.0 / (1.0 + jnp.exp(-x))


def _rms(x, g, n=None):
    n = x.shape[-1] if n is None else n
    ms = jnp.sum(x * x, axis=-1, keepdims=True) * (1.0 / n)
    return x * lax.rsqrt(ms + NORM_EPS) * g


def _rotate(x, cos, sin_signed, half, width):
    lane = lax.broadcasted_iota(jnp.int32, x.shape, x.ndim - 1)
    first = (lane % (2 * half)) < half
    partner = jnp.where(first, pltpu.roll(x, width - half, x.ndim - 1), pltpu.roll(x, half, x.ndim - 1))
    return x * cos + partner * sin_signed


def _norm_matmul_kernel(x_ref, g_ref, w_ref, o_ref):
    h = _rms(x_ref[...], g_ref[...])
    o_ref[...] = jnp.dot(h.astype(BF16), w_ref[...], preferred_element_type=F32)


def _norm_matmul(x, g, w, tm):
    rows = x.shape[0]
    return pl.pallas_call(
        _norm_matmul_kernel,
        grid=(P_W // P_TN, rows // tm),
        in_specs=[pl.BlockSpec((tm, D_MODEL), lambda j, i: (i, 0)),
                  pl.BlockSpec((1, D_MODEL), lambda j, i: (0, 0)),
                  pl.BlockSpec((D_MODEL, P_TN), lambda j, i: (0, j))],
        out_specs=pl.BlockSpec((tm, P_TN), lambda j, i: (i, j)),
        out_shape=jax.ShapeDtypeStruct((rows, P_W), F32),
        compiler_params=_cparams(("parallel", "parallel")),
        name="norm_in_proj",
    )(x, g, w)


def _rwkv_prep_kernel(pa_ref, prev_ref, mu_ref, w0_ref, w2_ref, a0_ref, a2_ref, g2_ref, kkp_ref,
                      ka_ref, bd_ref, r_o, w_o, kh_o, v_o, kk_o, b_o, g_o, *, shifted, tiles_per_seq):
    pa = pa_ref[...]
    if shifted:
        i = pl.program_id(0)
        rolled = pltpu.roll(pa, 1, 0)
        halo = prev_ref[7:8, :]
        halo = jnp.where(i % tiles_per_seq == 0, jnp.zeros_like(halo), halo)
        row = lax.broadcasted_iota(jnp.int32, pa.shape, 0)
        prev = jnp.where(row == 0, halo, rolled)
    else:
        prev = prev_ref[...]
    xm = pa + (prev - pa) * mu_ref[...]
    r = xm[:, 0:256]
    k = xm[:, 256:512]
    v = xm[:, 512:768]
    wl = xm[:, 768:832]
    al = xm[:, 832:896]
    gl = xm[:, 896:1024]
    logw = -DECAY_SCALE * _sigmoid(w0_ref[...] + _bdot(jnp.tanh(wl), w2_ref[...]))
    a = _sigmoid(a0_ref[...] + _bdot(al, a2_ref[...]))
    g = _bdot(_sigmoid(gl), g2_ref[...])
    kk = k * kkp_ref[...]
    ss = _dot01(kk * kk, bd_ref[...])
    kk = kk / jnp.maximum(jnp.sqrt(ss), 1e-12)
    kh = k * (1.0 + (a - 1.0) * ka_ref[...])
    b = kk * a
    for h in range(N_HEADS):
        sl = slice(h * HEAD_DIM, (h + 1) * HEAD_DIM)
        r_o[h] = r[:, sl]
        w_o[h] = logw[:, sl]
        kh_o[h] = kh[:, sl]
        v_o[h] = v[:, sl]
        kk_o[h] = kk[:, sl]
        b_o[h] = b[:, sl]
    g_o[...] = g


def _rwkv_prep(p, prev, lw, bd, tm, shifted, seq_len):
    rows = p.shape[0]
    tiles_per_seq = max(seq_len // tm, 1)
    col = P_RWKV // RWKV_IN
    if shifted:
        prev_arr = p
        prev_spec = pl.BlockSpec((8, RWKV_IN), lambda i: (jnp.maximum(i * (tm // 8) - 1, 0), col))
    else:
        prev_arr = prev
        prev_spec = pl.BlockSpec((tm, RWKV_IN), lambda i: (i, 0))
    vec = lambda n: pl.BlockSpec((1, n), lambda i: (0, 0))
    mat = lambda a, b: pl.BlockSpec((a, b), lambda i: (0, 0))
    hm = jax.ShapeDtypeStruct((N_HEADS, rows, HEAD_DIM), F32)
    hm_spec = pl.BlockSpec((N_HEADS, tm, HEAD_DIM), lambda i: (0, i, 0))
    return pl.pallas_call(
        functools.partial(_rwkv_prep_kernel, shifted=shifted, tiles_per_seq=tiles_per_seq),
        grid=(rows // tm,),
        in_specs=[pl.BlockSpec((tm, RWKV_IN), lambda i: (i, col)), prev_spec,
                  vec(RWKV_IN), vec(256), mat(DECAY_LORA, 256), vec(256), mat(ICLR_LORA, 256),
                  mat(GATE_LORA, 256), vec(256), vec(256), mat(256, 256)],
        out_specs=[hm_spec] * 6 + [pl.BlockSpec((tm, 256), lambda i: (i, 0))],
        out_shape=[hm] * 6 + [jax.ShapeDtypeStruct((rows, 256), F32)],
        compiler_params=_cparams(("parallel",)),
        name="rwkv_prep",
    )(p, prev_arr, lw["mu"], lw["w0"], lw["w2"], lw["a0"], lw["a2"], lw["g2"], lw["kkp"], lw["ka"], bd)


def _rwkv_step(s, r, w, kh, v, kk, b, eye):
    sa = jnp.sum(s * kk, axis=-1, keepdims=True)
    v_col = jnp.sum(jnp.where(eye, v, 0.0), axis=-1, keepdims=True)
    s = s * w - sa * b + v_col * kh
    y_col = jnp.sum(s * r, axis=-1, keepdims=True)
    y_row = jnp.sum(jnp.where(eye, y_col, 0.0), axis=0, keepdims=True)
    return s, y_row


def _eye64():
    return (lax.broadcasted_iota(jnp.int32, (HEAD_DIM, HEAD_DIM), 0)
            == lax.broadcasted_iota(jnp.int32, (HEAD_DIM, HEAD_DIM), 1))


def _dot2(a, b, dims=(((1,), (0,)), ((), ()))):
    ah = a.astype(BF16)
    al = (a - ah.astype(F32)).astype(BF16)
    bh = b.astype(BF16)
    bl = (b - bh.astype(F32)).astype(BF16)
    d = lambda x, y: lax.dot_general(x, y, dims, preferred_element_type=F32)
    return d(ah, bh) + (d(ah, bl) + d(al, bh))


_NT = (((1,), (1,)), ((), ()))
_TN = (((0,), (0,)), ((), ()))
RWKV_CHUNK = 64


def _rwkv_chunk_kernel(r_ref, lw_ref, kh_ref, v_ref, kk_ref, b_ref, tri_ref, y_ref, sfin_ref, s_scr):
    c = pl.program_id(1)

    @pl.when(c == 0)
    def _():
        s_scr[...] = jnp.zeros_like(s_scr)

    C = RWKV_CHUNK
    row = lax.broadcasted_iota(jnp.int32, (C, 2 * C), 0)
    col = lax.broadcasted_iota(jnp.int32, (C, 2 * C), 1)
    left = col < C
    colr = jnp.where(left, col, col - C)
    strict = row > colr
    incl = row >= colr
    eye = (row == col)[:, 0:C]
    tri = tri_ref[...]
    heads = range(s_scr.shape[0])
    at = lambda ref, h: ref.at[h % N_HEADS, h // N_HEADS]
    lw, r, kh, v, kk, b = ([at(ref, h)[...] for h in heads]
                           for ref in (lw_ref, r_ref, kh_ref, v_ref, kk_ref, b_ref))
    lp = [_dot01_left(tri, lw[h]) for h in heads]
    lp_end = [lp[h][C - 1:C, :] for h in heads]
    kap = [kk[h] * jnp.exp(lp[h] - lw[h]) for h in heads]
    rt = [r[h] * jnp.exp(lp[h]) for h in heads]
    e_in = [jnp.exp(-lp[h]) for h in heads]
    bk = [jnp.concatenate([b[h] * e_in[h], kh[h] * e_in[h]], axis=0) for h in heads]
    rem = [jnp.exp(lp_end[h] - lp[h]) for h in heads]
    g1 = [_dot2(kap[h], bk[h], _NT) for h in heads]
    g2 = [_bdot_nt(rt[h], bk[h]) for h in heads]
    l = [jnp.where(strict[:, 0:C], g1[h][:, 0:C], 0.0) for h in heads]
    x = [jnp.where(eye, 1.0, 0.0) - l[h] for h in heads]
    p = [_dot2(l[h], l[h]) for h in heads]
    for i in range(5):
        x = [x[h] + _dot2(x[h], p[h]) for h in heads]
        if i < 4:
            p = [_dot2(p[h], p[h]) for h in heads]
    s0 = [s_scr[h] for h in heads]
    not_left_strict = jnp.logical_and(jnp.logical_not(left), strict)
    rhs = [_dot2(kap[h], s0[h], _NT)
           + _dot2(jnp.where(not_left_strict, g1[h], 0.0), jnp.concatenate([v[h], v[h]], axis=0)) for h in heads]
    u = [_dot2(x[h], rhs[h]) for h in heads]
    for h in heads:
        g2m = jnp.where(incl, jnp.where(left, -g2[h], g2[h]), 0.0)
        at(y_ref, h)[...] = _bdot_nt(rt[h], s0[h]) + _bdot(g2m, jnp.concatenate([u[h], v[h]], axis=0))
    for h in heads:
        s_scr[h] = (s0[h] * jnp.exp(lp_end[h]) + _dot2(v[h], kh[h] * rem[h], _TN)
                    - _dot2(u[h], b[h] * rem[h], _TN))

    @pl.when(c == pl.num_programs(1) - 1)
    def _():
        sfin_ref[...] = s_scr[...].reshape(sfin_ref.shape)


def _rwkv_scan(r, lw, kh, v, kk, b, nseq, seq_len):
    C = RWKV_CHUNK
    sp = 2 if nseq % 2 == 0 else 1
    view = lambda a: a.reshape(N_HEADS, nseq, seq_len, HEAD_DIM)
    spec = pl.BlockSpec((N_HEADS, sp, C, HEAD_DIM), lambda n, c: (0, n, c, 0))
    tri = (jnp.arange(C)[:, None] >= jnp.arange(C)[None, :]).astype(BF16)
    y, sfin = pl.pallas_call(
        _rwkv_chunk_kernel,
        grid=(nseq // sp, seq_len // C),
        in_specs=[spec] * 6 + [pl.BlockSpec((C, C), lambda n, c: (0, 0))],
        out_specs=[spec, pl.BlockSpec((sp, N_HEADS, HEAD_DIM, HEAD_DIM), lambda n, c: (n, 0, 0, 0))],
        out_shape=[jax.ShapeDtypeStruct((N_HEADS, nseq, seq_len, HEAD_DIM), F32),
                   jax.ShapeDtypeStruct((nseq, N_HEADS, HEAD_DIM, HEAD_DIM), F32)],
        scratch_shapes=[pltpu.VMEM((sp * N_HEADS, HEAD_DIM, HEAD_DIM), F32)],
        compiler_params=_cparams(("parallel", "arbitrary")),
        name="rwkv_chunks",
    )(view(r), view(lw), view(kh), view(v), view(kk), view(b), tri)
    return y.reshape(N_HEADS, nseq * seq_len, HEAD_DIM), sfin


def _rwkv_single_kernel(r_ref, w_ref, kh_ref, v_ref, kk_ref, b_ref, s_ref, y_ref, so_ref, *, bs):
    eye = _eye64()
    for n in range(bs):
        for h in range(N_HEADS):
            row = lambda ref: ref[h, n:n + 1, :]
            s, y_row = _rwkv_step(s_ref[n, h], row(r_ref), jnp.exp(row(w_ref)), row(kh_ref), row(v_ref),
                                  row(kk_ref), row(b_ref), eye)
            so_ref[n, h] = s
            y_ref[h, n:n + 1, :] = y_row


def _rwkv_single(r, w, kh, v, kk, b, s0, bs=8):
    nb = s0.shape[0]
    spec = pl.BlockSpec((N_HEADS, bs, HEAD_DIM), lambda i: (0, i, 0))
    s_spec = pl.BlockSpec((bs, N_HEADS, HEAD_DIM, HEAD_DIM), lambda i: (i, 0, 0, 0))
    return pl.pallas_call(
        functools.partial(_rwkv_single_kernel, bs=bs),
        grid=(nb // bs,),
        in_specs=[spec] * 6 + [s_spec],
        out_specs=[spec, s_spec],
        out_shape=[jax.ShapeDtypeStruct((N_HEADS, nb, HEAD_DIM), F32),
                   jax.ShapeDtypeStruct(s0.shape, F32)],
        compiler_params=_cparams(("parallel",)),
        name="rwkv_single",
    )(r, w, kh, v, kk, b, s0)


def _rwkv_post_kernel(y_ref, r_ref, kh_ref, v_ref, g_ref, rk_ref, lng_ref, lnb_ref, o_ref):
    outs = []
    for h in range(N_HEADS):
        bonus = jnp.sum(r_ref[h] * kh_ref[h] * rk_ref[h], axis=-1, keepdims=True)
        y = y_ref[h] + bonus * v_ref[h]
        mu = jnp.mean(y, axis=-1, keepdims=True)
        d = y - mu
        var = jnp.mean(d * d, axis=-1, keepdims=True)
        outs.append(d * lax.rsqrt(var + GN_EPS))
    y = jnp.concatenate(outs, axis=-1) * lng_ref[...] + lnb_ref[...]
    o_ref[...] = y * g_ref[...]


def _rwkv_post(y, r, kh, v, g, lw, tm):
    rows = g.shape[0]
    hm_spec = pl.BlockSpec((N_HEADS, tm, HEAD_DIM), lambda i: (0, i, 0))
    vec = pl.BlockSpec((1, 256), lambda i: (0, 0))
    return pl.pallas_call(
        _rwkv_post_kernel,
        grid=(rows // tm,),
        in_specs=[hm_spec] * 4 + [pl.BlockSpec((tm, 256), lambda i: (i, 0)),
                                  pl.BlockSpec((N_HEADS, 1, HEAD_DIM), lambda i: (0, 0, 0)), vec, vec],
        out_specs=pl.BlockSpec((tm, 256), lambda i: (i, 0)),
        out_shape=jax.ShapeDtypeStruct((rows, 256), F32),
        compiler_params=_cparams(("parallel",)),
        name="rwkv_post",
    )(y, r, kh, v, g, lw["rk"], lw["ln_g"], lw["ln_b"])


def _ret_log_decay():
    return jnp.log1p(-jnp.exp2(-5.0 - jnp.arange(N_HEADS, dtype=F32)))


def _rope_tables(pos, half, reps):
    inv = ROPE_BASE ** (-jnp.arange(half, dtype=F32) / half)
    ang = pos.astype(F32)[:, None] * inv[None, :]
    cos, sin = jnp.cos(ang), jnp.sin(ang)
    cos_t = jnp.tile(jnp.concatenate([cos, cos], axis=1), (1, reps))
    sin_t = jnp.tile(jnp.concatenate([-sin, sin], axis=1), (1, reps))
    return cos_t, sin_t


def _ret_norm_gate(outs, g):
    o = jnp.concatenate([x * lax.rsqrt(jnp.mean(x * x, axis=-1, keepdims=True) + NORM_EPS) for x in outs],
                        axis=-1)
    return (g * _sigmoid(g)) * o


def _ret_chunk_kernel(pb_ref, cos_ref, sin_ref, dmask_ref, dq_ref, dk_ref, dc_ref, o_ref, sfin_ref, s_scr):
    c = pl.program_id(1)

    @pl.when(c == 0)
    def _():
        s_scr[...] = jnp.zeros_like(s_scr)

    pb = pb_ref[...]
    cos, sin = cos_ref[...], sin_ref[...]
    q = _rotate(pb[:, 0:256], cos, sin, 32, 256)
    k = _rotate(pb[:, 256:512], cos, sin, 32, 256) * (HEAD_DIM ** -0.5)
    v = pb[:, 512:768]
    qd = q * dq_ref[...]
    kd = k * dk_ref[...]
    dc = dc_ref[...]
    outs = []
    for h in range(N_HEADS):
        sl = slice(h * HEAD_DIM, (h + 1) * HEAD_DIM)
        qk = _bdot_nt(q[:, sl], k[:, sl]) * dmask_ref[h]
        s = s_scr[h]
        outs.append(_bdot(qk, v[:, sl]) + _bdot(qd[:, sl], s))
        s_scr[h] = s * dc[h] + _bdot_tn(kd[:, sl], v[:, sl])
    o_ref[...] = _ret_norm_gate(outs, pb[:, 768:1024])

    @pl.when(c == pl.num_programs(1) - 1)
    def _():
        sfin_ref[...] = s_scr[...]


def _ret_prompt(p, nseq, seq_len):
    lg = _ret_log_decay()
    t = jnp.arange(RET_CHUNK, dtype=F32)
    diff = t[:, None] - t[None, :]
    dmask = jnp.where(diff >= 0, jnp.exp(lg[:, None, None] * jnp.maximum(diff, 0.0)), 0.0)
    lanes = lambda a: jnp.repeat(a, HEAD_DIM, axis=-1)
    dq = lanes(jnp.exp(lg[None, :] * (t[:, None] + 1.0)))
    dk = lanes(jnp.exp(lg[None, :] * (RET_CHUNK - 1.0 - t[:, None])))
    dc = jnp.broadcast_to(jnp.exp(lg * RET_CHUNK)[:, None, None], (N_HEADS, 1, HEAD_DIM))
    cos, sin = _rope_tables(jnp.arange(seq_len), 32, N_HEADS)
    nc = seq_len // RET_CHUNK
    full = lambda *s: pl.BlockSpec(s, lambda n, c: (0,) * len(s))
    return pl.pallas_call(
        _ret_chunk_kernel,
        grid=(nseq, nc),
        in_specs=[pl.BlockSpec((RET_CHUNK, 1024), lambda n, c: (n * nc + c, P_RET // 1024)),
                  pl.BlockSpec((RET_CHUNK, 256), lambda n, c: (c, 0)),
                  pl.BlockSpec((RET_CHUNK, 256), lambda n, c: (c, 0)),
                  full(N_HEADS, RET_CHUNK, RET_CHUNK), full(RET_CHUNK, 256), full(RET_CHUNK, 256),
                  full(N_HEADS, 1, HEAD_DIM)],
        out_specs=[pl.BlockSpec((RET_CHUNK, 256), lambda n, c: (n * nc + c, 0)),
                   pl.BlockSpec((None, N_HEADS, HEAD_DIM, HEAD_DIM), lambda n, c: (n, 0, 0, 0))],
        out_shape=[jax.ShapeDtypeStruct((nseq * seq_len, 256), F32),
                   jax.ShapeDtypeStruct((nseq, N_HEADS, HEAD_DIM, HEAD_DIM), F32)],
        scratch_shapes=[pltpu.VMEM((N_HEADS, HEAD_DIM, HEAD_DIM), F32)],
        compiler_params=_cparams(("parallel", "arbitrary")),
        name="retention_chunks",
    )(p, cos, sin, dmask, dq, dk, dc)


def _ret_single_kernel(pb_ref, cos_ref, sin_ref, dec_ref, s_ref, o_ref, so_ref, *, bs):
    pb = pb_ref[...]
    cos, sin = cos_ref[...], sin_ref[...]
    q = _rotate(pb[:, 0:256], cos, sin, 32, 256)
    k = _rotate(pb[:, 256:512], cos, sin, 32, 256) * (HEAD_DIM ** -0.5)
    v = pb[:, 512:768]
    eye = _eye64()
    dec = dec_ref[...]
    rows = []
    for n in range(bs):
        outs = []
        for h in range(N_HEADS):
            sl = slice(h * HEAD_DIM, (h + 1) * HEAD_DIM)
            qr, kr, vr = q[n:n + 1, sl], k[n:n + 1, sl], v[n:n + 1, sl]
            q_col = jnp.sum(jnp.where(eye, qr, 0.0), axis=-1, keepdims=True)
            k_col = jnp.sum(jnp.where(eye, kr, 0.0), axis=-1, keepdims=True)
            s = s_ref[n, h]
            d = dec[h]
            o = jnp.sum(qr * kr, axis=-1, keepdims=True) * vr + jnp.sum((q_col * d) * s, axis=0, keepdims=True)
            so_ref[n, h] = s * d + k_col * vr
            outs.append(o)
        rows.append(jnp.concatenate(outs, axis=-1))
    o = jnp.concatenate(rows, axis=0)
    g = pb[:, 768:1024]
    o_ref[...] = _ret_norm_gate([o[:, h * HEAD_DIM:(h + 1) * HEAD_DIM] for h in range(N_HEADS)], g)


def _ret_single(p, s0, pos, bs=8):
    nb = s0.shape[0]
    cos, sin = _rope_tables(jnp.full((1,), pos), 32, N_HEADS)
    dec = jnp.broadcast_to(jnp.exp(_ret_log_decay())[:, None, None], (N_HEADS, 1, HEAD_DIM))
    s_spec = pl.BlockSpec((bs, N_HEADS, HEAD_DIM, HEAD_DIM), lambda i: (i, 0, 0, 0))
    return pl.pallas_call(
        functools.partial(_ret_single_kernel, bs=bs),
        grid=(nb // bs,),
        in_specs=[pl.BlockSpec((bs, 1024), lambda i: (i, P_RET // 1024)),
                  pl.BlockSpec((1, 256), lambda i: (0, 0)), pl.BlockSpec((1, 256), lambda i: (0, 0)),
                  pl.BlockSpec((N_HEADS, 1, HEAD_DIM), lambda i: (0, 0, 0)), s_spec],
        out_specs=[pl.BlockSpec((bs, 256), lambda i: (i, 0)), s_spec],
        out_shape=[jax.ShapeDtypeStruct((nb, 256), F32), jax.ShapeDtypeStruct(s0.shape, F32)],
        compiler_params=_cparams(("parallel",)),
        name="retention_single",
    )(p, cos, sin, dec, s0)


def _mla_prep_kernel(cq_ref, ckv_ref, kr_ref, cos_ref, sin_ref, qng_ref, kvg_ref, wn_ref, wr_ref, wuk_ref,
                     qf_ref, rows_ref):
    cqn = _rms(cq_ref[...], qng_ref[...], n=Q_LORA)
    q_nope = _bdot(cqn, wn_ref[...]) * MLA_SCALE
    q_rope = _bdot(cqn, wr_ref[...]) * MLA_SCALE
    q_lat = _bdot(q_nope, wuk_ref[...])
    cos, sin = cos_ref[...], sin_ref[...]
    q_rope = _rotate(q_rope, cos, sin, 16, 128)
    kr = _rotate(kr_ref[...], cos, sin, 16, 128)
    lane = lax.broadcasted_iota(jnp.int32, kr.shape, 1)
    keep = lane < ROPE_DIM
    rows_ref[:, 0:KV_LORA] = _rms(ckv_ref[...], kvg_ref[...])
    rows_ref[:, KV_LORA:MLA_WP] = jnp.where(keep, kr, 0.0)
    for h in range(N_HEADS):
        qf_ref[h, :, 0:KV_LORA] = q_lat[:, h * KV_LORA:(h + 1) * KV_LORA]
        shifted = q_rope if h == 0 else pltpu.roll(q_rope, 128 - h * ROPE_DIM, 1)
        qf_ref[h, :, KV_LORA:MLA_WP] = jnp.where(keep, shifted, 0.0)


def _mla_prep(p, lw, pos, tm, tiles_per_seq):
    rows = p.shape[0]
    cos, sin = _rope_tables(pos, 16, 4)
    if pos.shape[0] == 1:
        tab = pl.BlockSpec((1, 128), lambda i: (0, 0))
    else:
        tab = pl.BlockSpec((tm, 128), lambda i: (i % tiles_per_seq, 0))
    full = lambda a, b: pl.BlockSpec((a, b), lambda i: (0, 0))
    return pl.pallas_call(
        _mla_prep_kernel,
        grid=(rows // tm,),
        in_specs=[pl.BlockSpec((tm, 256), lambda i: (i, P_CQ // 256)),
                  pl.BlockSpec((tm, 256), lambda i: (i, P_CKV // 256)),
                  pl.BlockSpec((tm, 128), lambda i: (i, P_KRFL // 128)),
                  tab, tab, full(1, 256), full(1, 256), full(256, 256), full(256, 128), full(256, 1024)],
        out_specs=[pl.BlockSpec((N_HEADS, tm, MLA_WP), lambda i: (0, i, 0)),
                   pl.BlockSpec((tm, MLA_WP), lambda i: (i, 0))],
        out_shape=[jax.ShapeDtypeStruct((N_HEADS, rows, MLA_WP), F32),
                   jax.ShapeDtypeStruct((rows, MLA_WP), F32)],
        compiler_params=_cparams(("parallel",)),
        name="mla_prep",
    )(p, p, p, cos, sin, lw["qn_g"], lw["kvn_g"], lw["w_nope"], lw["w_rope"], lw["wuk_bd"])


def _tri_schedule(n):
    qi = np.concatenate([np.full((i + 1,), i, np.int32) for i in range(n)])
    kj = np.concatenate([np.arange(i + 1, dtype=np.int32) for i in range(n)])
    return jnp.asarray(qi), jnp.asarray(kj)


def _online_softmax(s, m_ref, l_ref):
    m_prev = m_ref[...]
    m_new = jnp.maximum(m_prev, jnp.max(s, axis=-1, keepdims=True))
    alpha = jnp.exp(m_prev - m_new)
    p = jnp.exp(s - m_new)
    l_ref[...] = alpha * l_ref[...] + jnp.sum(p, axis=-1, keepdims=True)
    m_ref[...] = m_new
    return p, alpha


def _mla_flash_kernel(qi_ref, kj_ref, q_ref, kv_ref, wuv_ref, o_ref, m_scr, l_scr, acc_scr, *, tq):
    step = pl.program_id(1)
    i, j = qi_ref[step], kj_ref[step]

    @pl.when(j == 0)
    def _():
        m_scr[...] = jnp.full_like(m_scr, NEG)
        l_scr[...] = jnp.zeros_like(l_scr)
        acc_scr[...] = jnp.zeros_like(acc_scr)

    def block(diagonal):
        q = q_ref[...].reshape(N_HEADS * tq, MLA_WP)
        kv = kv_ref[...].astype(BF16)
        s = _bdot_nt(q, kv)
        if diagonal:
            row = lax.broadcasted_iota(jnp.int32, s.shape, 0) % tq
            col = lax.broadcasted_iota(jnp.int32, s.shape, 1)
            s = jnp.where(col <= row, s, NEG)
        p, alpha = _online_softmax(s, m_scr, l_scr)
        acc_scr[...] = alpha * acc_scr[...] + jnp.dot(p.astype(BF16), kv[:, 0:KV_LORA],
                                                      preferred_element_type=F32)

    @pl.when(j < i)
    def _():
        block(False)

    @pl.when(j == i)
    def _():
        block(True)
        o = acc_scr[...] / l_scr[...]
        o_ref[...] = jnp.concatenate(
            [_bdot(o[h * tq:(h + 1) * tq], wuv_ref[h]) for h in range(N_HEADS)], axis=-1)


def _mla_flash(qf, rows, wuv, nseq, seq_len, tq):
    nq = seq_len // tq
    qi, kj = _tri_schedule(nq)
    grid_spec = pltpu.PrefetchScalarGridSpec(
        num_scalar_prefetch=2,
        grid=(nseq, int(qi.shape[0])),
        in_specs=[pl.BlockSpec((N_HEADS, tq, MLA_WP), lambda n, s, qi, kj: (0, n * nq + qi[s], 0)),
                  pl.BlockSpec((tq, MLA_WP), lambda n, s, qi, kj: (n * nq + kj[s], 0)),
                  pl.BlockSpec((N_HEADS, KV_LORA, HEAD_DIM), lambda n, s, qi, kj: (0, 0, 0))],
        out_specs=pl.BlockSpec((tq, 256), lambda n, s, qi, kj: (n * nq + qi[s], 0)),
        scratch_shapes=[pltpu.VMEM((N_HEADS * tq, 1), F32), pltpu.VMEM((N_HEADS * tq, 1), F32),
                        pltpu.VMEM((N_HEADS * tq, KV_LORA), F32)])
    return pl.pallas_call(
        functools.partial(_mla_flash_kernel, tq=tq),
        grid_spec=grid_spec,
        out_shape=jax.ShapeDtypeStruct((nseq * seq_len, 256), F32),
        compiler_params=_cparams(("parallel", "arbitrary")),
        name="mla_prompt_attention",
    )(qi, kj, qf, rows, wuv)


def _mla_decode_kernel(pt_ref, q_ref, new_ref, wuv_ref, c_hbm, o_ref, kbuf, sem, m_scr, l_scr, acc_scr,
                       *, layer, pg, n_pages):
    b = pl.program_id(0)
    n_groups = n_pages // pg

    def page_copies(seq, grp, slot):
        return [pltpu.make_async_copy(c_hbm.at[layer, pt_ref[seq, grp * pg + i]], kbuf.at[slot, i], sem.at[slot])
                for i in range(pg)]

    @pl.when(b == 0)
    def _():
        for cp in page_copies(0, 0, 0):
            cp.start()

    m_scr[...] = jnp.full_like(m_scr, NEG)
    l_scr[...] = jnp.zeros_like(l_scr)
    acc_scr[...] = jnp.zeros_like(acc_scr)

    def group(grp, carry):
        slot = (b * n_groups + grp) % 2

        @pl.when(grp + 1 < n_groups)
        def _():
            for cp in page_copies(b, grp + 1, 1 - slot):
                cp.start()

        @pl.when(jnp.logical_and(grp + 1 == n_groups, b + 1 < pl.num_programs(0)))
        def _():
            for cp in page_copies(b + 1, 0, 1 - slot):
                cp.start()

        for cp in page_copies(b, grp, slot):
            cp.wait()

        q = q_ref[...].astype(BF16)
        q_lat, q_rope = q[:, 0:KV_LORA], q[:, KV_LORA:MLA_W]
        kts = [kbuf[slot, i].astype(BF16) for i in range(pg)]
        s = jnp.concatenate(
            [jnp.dot(q_lat, kt[0:KV_LORA], preferred_element_type=F32)
             + jnp.dot(q_rope, kt[KV_LORA:MLA_W], preferred_element_type=F32)
             for kt in kts], axis=-1)
        p, alpha = _online_softmax(s, m_scr, l_scr)
        p = p.astype(BF16)
        acc = alpha * acc_scr[...]
        for i, kt in enumerate(kts):
            acc = acc + lax.dot_general(p[:, i * PAGE:(i + 1) * PAGE], kt[0:KV_LORA], _NT,
                                        preferred_element_type=F32)
        acc_scr[...] = acc
        return carry

    lax.fori_loop(0, n_groups, group, 0)

    new = new_ref[...]
    s_new = jnp.sum(q_ref[...] * new, axis=-1, keepdims=True)
    m_prev = m_scr[...]
    m_new = jnp.maximum(m_prev, s_new)
    a = jnp.exp(m_prev - m_new)
    p_new = jnp.exp(s_new - m_new)
    l = a * l_scr[...] + p_new
    o = (a * acc_scr[...] + p_new * new[:, 0:KV_LORA]) / l
    o_ref[...] = jnp.concatenate([_bdot(o[h:h + 1], wuv_ref[h]) for h in range(N_HEADS)], axis=-1)


def _mla_decode(cache, layer, page_table, q, new_rows, wuv, pg):
    nb, n_pages = page_table.shape
    grid_spec = pltpu.PrefetchScalarGridSpec(
        num_scalar_prefetch=1,
        grid=(nb,),
        in_specs=[pl.BlockSpec((None, 8, MLA_WP), lambda b, pt: (b, 0, 0)),
                  pl.BlockSpec((None, 1, MLA_WP), lambda b, pt: (b, 0, 0)),
                  pl.BlockSpec((N_HEADS, KV_LORA, HEAD_DIM), lambda b, pt: (0, 0, 0)),
                  pl.BlockSpec(memory_space=pl.ANY)],
        out_specs=pl.BlockSpec((None, 1, 256), lambda b, pt: (b, 0, 0)),
        scratch_shapes=[pltpu.VMEM((2, pg, MLA_W, PAGE), F32), pltpu.SemaphoreType.DMA((2,)),
                        pltpu.VMEM((8, 1), F32), pltpu.VMEM((8, 1), F32), pltpu.VMEM((8, KV_LORA), F32)])
    out = pl.pallas_call(
        functools.partial(_mla_decode_kernel, layer=layer, pg=pg, n_pages=n_pages),
        grid_spec=grid_spec,
        out_shape=jax.ShapeDtypeStruct((nb, 1, 256), F32),
        compiler_params=_cparams(("arbitrary",)),
        name="mla_paged_attention",
    )(page_table, q, new_rows, wuv, cache)
    return out.reshape(nb, 256)


def _log_sigmoid(x):
    return jnp.minimum(x, 0.0) - jnp.log(1.0 + jnp.exp(-jnp.abs(x)))


def _fox_prep_kernel(fq_ref, fk_ref, fv_ref, fl_ref, bf_ref, tri_ref, q_o, k_o, v_o, lf_o, c_o, ct_o, carry,
                     *, cumulative):
    fq = fq_ref[...] * FOX_SCALE
    fk, fv = fk_ref[...], fv_ref[...]
    for h in range(N_HEADS):
        q_o[h] = fq[:, h * HEAD_DIM:(h + 1) * HEAD_DIM]
    for g in range(2):
        k_o[g] = fk[:, g * HEAD_DIM:(g + 1) * HEAD_DIM]
        v_o[g] = fv[:, g * HEAD_DIM:(g + 1) * HEAD_DIM]
    x = pltpu.roll(fl_ref[...] + bf_ref[...], 128 - ROPE_DIM, 1)
    lane = lax.broadcasted_iota(jnp.int32, x.shape, 1)
    lf = jnp.where(lane < N_HEADS, _log_sigmoid(x), 0.0)
    lf_o[...] = lf
    if cumulative:
        @pl.when(pl.program_id(1) == 0)
        def _():
            carry[...] = jnp.zeros_like(carry)
        c = _dot01_left(tri_ref[...], lf) + carry[...]
        carry[...] = c[c.shape[0] - 1:c.shape[0], :]
    else:
        c = lf
    c_o[...] = c
    ct_o[...] = jnp.transpose(c)[0:8, :]


def _fox_prep(p, bf, tm, nseq, seq_len, cumulative):
    rows = p.shape[0]
    tps = seq_len // tm if cumulative else rows // tm
    gn = nseq if cumulative else 1
    tri = (jnp.arange(tm)[:, None] >= jnp.arange(tm)[None, :]).astype(BF16)
    idx = lambda w, off: (lambda n, i: (n * tps + i, off // w))
    out_idx = lambda n, i: (n * tps + i, 0)
    hm = lambda k: pl.BlockSpec((k, tm, HEAD_DIM), lambda n, i: (0, n * tps + i, 0))
    return pl.pallas_call(
        functools.partial(_fox_prep_kernel, cumulative=cumulative),
        grid=(gn, tps),
        in_specs=[pl.BlockSpec((tm, 256), idx(256, P_FQ)), pl.BlockSpec((tm, 128), idx(128, P_FK)),
                  pl.BlockSpec((tm, 128), idx(128, P_FV)), pl.BlockSpec((tm, 128), idx(128, P_KRFL)),
                  pl.BlockSpec((1, 128), lambda n, i: (0, 0)), pl.BlockSpec((tm, tm), lambda n, i: (0, 0))],
        out_specs=[hm(4), hm(2), hm(2), pl.BlockSpec((tm, 128), out_idx), pl.BlockSpec((tm, 128), out_idx),
                   pl.BlockSpec((8, tm), lambda n, i: (0, n * tps + i))],
        out_shape=[jax.ShapeDtypeStruct((4, rows, HEAD_DIM), F32), jax.ShapeDtypeStruct((2, rows, HEAD_DIM), F32),
                   jax.ShapeDtypeStruct((2, rows, HEAD_DIM), F32), jax.ShapeDtypeStruct((rows, 128), F32),
                   jax.ShapeDtypeStruct((rows, 128), F32), jax.ShapeDtypeStruct((8, rows), F32)],
        scratch_shapes=[pltpu.VMEM((1, 128), F32)],
        compiler_params=_cparams(("parallel", "arbitrary")),
        name="fox_prep",
    )(p, p, p, p, bf, tri)


def _fox_flash_kernel(qi_ref, kj_ref, q_ref, k_ref, v_ref, c_ref, ct_ref, o_ref, m_scr, l_scr, acc_scr, *, tq):
    step = pl.program_id(1)
    i, j = qi_ref[step], kj_ref[step]

    @pl.when(j == 0)
    def _():
        m_scr[...] = jnp.full_like(m_scr, NEG)
        l_scr[...] = jnp.zeros_like(l_scr)
        acc_scr[...] = jnp.zeros_like(acc_scr)

    def block(diagonal):
        heads = range(N_HEADS)
        c = c_ref[...]
        ss = [_bdot_nt(q_ref[h], k_ref[h // 2]) + (c[:, h:h + 1] - ct_ref[h:h + 1, :]) for h in heads]
        if diagonal:
            row = lax.broadcasted_iota(jnp.int32, (tq, tq), 0)
            col = lax.broadcasted_iota(jnp.int32, (tq, tq), 1)
            ss = [jnp.where(col <= row, s, NEG) for s in ss]
        pa = [_online_softmax(ss[h], m_scr.at[h], l_scr.at[h]) for h in heads]
        for h in heads:
            p, alpha = pa[h]
            acc_scr[h] = alpha * acc_scr[h] + _bdot(p, v_ref[h // 2])

    @pl.when(j < i)
    def _():
        block(False)

    @pl.when(j == i)
    def _():
        block(True)
        o_ref[...] = jnp.concatenate([acc_scr[h] / l_scr[h] for h in range(N_HEADS)], axis=-1)


def _fox_flash(q, k, v, c, ct, nseq, seq_len, tq):
    nq = seq_len // tq
    qi, kj = _tri_schedule(nq)
    qmap = lambda n, s, qi, kj: (0, n * nq + qi[s], 0)
    kmap = lambda n, s, qi, kj: (0, n * nq + kj[s], 0)
    grid_spec = pltpu.PrefetchScalarGridSpec(
        num_scalar_prefetch=2,
        grid=(nseq, int(qi.shape[0])),
        in_specs=[pl.BlockSpec((4, tq, HEAD_DIM), qmap), pl.BlockSpec((2, tq, HEAD_DIM), kmap),
                  pl.BlockSpec((2, tq, HEAD_DIM), kmap),
                  pl.BlockSpec((tq, 128), lambda n, s, qi, kj: (n * nq + qi[s], 0)),
                  pl.BlockSpec((8, tq), lambda n, s, qi, kj: (0, n * nq + kj[s]))],
        out_specs=pl.BlockSpec((tq, 256), lambda n, s, qi, kj: (n * nq + qi[s], 0)),
        scratch_shapes=[pltpu.VMEM((4, tq, 1), F32), pltpu.VMEM((4, tq, 1), F32),
                        pltpu.VMEM((4, tq, HEAD_DIM), F32)])
    return pl.pallas_call(
        functools.partial(_fox_flash_kernel, tq=tq),
        grid_spec=grid_spec,
        out_shape=jax.ShapeDtypeStruct((nseq * seq_len, 256), F32),
        compiler_params=_cparams(("parallel", "arbitrary")),
        name="fox_prompt_attention",
    )(qi, kj, q, k, v, c, ct)


def _fox_decode_kernel(pt_ref, q_ref, kn_ref, vn_ref, cn_ref, slt_ref, k_hbm, v_hbm, lf_hbm, o_ref,
                       kbuf, vbuf, lfbuf, sem, m_scr, l_scr, acc_scr, run_scr, lf_scr, *, layer, pg, n_pages):
    b = pl.program_id(0)
    n_groups = n_pages // pg

    def page_copies(seq, grp, slot):
        out = []
        for i in range(pg):
            page = pt_ref[seq, n_pages - 1 - (grp * pg + i)]
            out.append(pltpu.make_async_copy(k_hbm.at[layer, page], kbuf.at[slot, i], sem.at[slot, 0]))
            out.append(pltpu.make_async_copy(v_hbm.at[layer, page], vbuf.at[slot, i], sem.at[slot, 1]))
            out.append(pltpu.make_async_copy(lf_hbm.at[layer, page], lfbuf.at[slot, i], sem.at[slot, 2]))
        return out

    @pl.when(b == 0)
    def _():
        for cp in page_copies(0, 0, 0):
            cp.start()
        lf_scr[...] = jnp.zeros_like(lf_scr)

    m_scr[...] = jnp.full_like(m_scr, NEG)
    l_scr[...] = jnp.zeros_like(l_scr)
    acc_scr[...] = jnp.zeros_like(acc_scr)
    run_scr[...] = jnp.zeros_like(run_scr)

    def group(grp, carry):
        slot = (b * n_groups + grp) % 2

        @pl.when(grp + 1 < n_groups)
        def _():
            for cp in page_copies(b, grp + 1, 1 - slot):
                cp.start()

        @pl.when(jnp.logical_and(grp + 1 == n_groups, b + 1 < pl.num_programs(0)))
        def _():
            for cp in page_copies(b + 1, 0, 1 - slot):
                cp.start()

        for cp in page_copies(b, grp, slot):
            cp.wait()

        for i in range(pg):
            lf_scr[i * 8:i * 8 + N_HEADS, :] = lfbuf[slot, i]
        lf = lf_scr[...]
        within = _dot01(lf, slt_ref[...])
        totals = jnp.sum(lf, axis=-1, keepdims=True)
        q = q_ref[...].astype(BF16)
        run = run_scr[...] + cn_ref[...]
        tiles = []
        for i in range(pg):
            kt = kbuf[slot, i].reshape(2 * HEAD_DIM, PAGE).astype(BF16)
            s = jnp.dot(q, kt, preferred_element_type=F32)
            tiles.append(s + within[i * 8:(i + 1) * 8, :] + run)
            run = run + totals[i * 8:(i + 1) * 8, :]
        run_scr[...] = run - cn_ref[...]
        s = jnp.concatenate(tiles, axis=-1)
        p, alpha = _online_softmax(s, m_scr, l_scr)
        p = p.astype(BF16)
        acc = alpha * acc_scr[...]
        for i in range(pg):
            vt = vbuf[slot, i].reshape(2 * HEAD_DIM, PAGE).astype(BF16)
            acc = acc + lax.dot_general(p[:, i * PAGE:(i + 1) * PAGE], vt, _NT, preferred_element_type=F32)
        acc_scr[...] = acc
        return carry

    lax.fori_loop(0, n_groups, group, 0)

    s_new = jnp.sum(q_ref[...] * kn_ref[...], axis=-1, keepdims=True)
    m_prev = m_scr[...]
    m_new = jnp.maximum(m_prev, s_new)
    a = jnp.exp(m_prev - m_new)
    p_new = jnp.exp(s_new - m_new)
    l = a * l_scr[...] + p_new
    o = (a * acc_scr[...] + p_new * vn_ref[...]) / l
    o_ref[...] = jnp.concatenate(
        [o[h:h + 1, (h // 2) * HEAD_DIM:(h // 2 + 1) * HEAD_DIM] for h in range(N_HEADS)], axis=-1)


def _fox_decode(cache_k, cache_v, cache_lf, layer, page_table, q8, k_new, v_new, cn8, pg):
    nb, n_pages = page_table.shape
    slt = (jnp.arange(PAGE)[:, None] > jnp.arange(PAGE)[None, :]).astype(BF16)
    per_seq = lambda r, w: pl.BlockSpec((None, r, w), lambda b, pt: (b, 0, 0))
    hbm = pl.BlockSpec(memory_space=pl.ANY)
    grid_spec = pltpu.PrefetchScalarGridSpec(
        num_scalar_prefetch=1,
        grid=(nb,),
        in_specs=[per_seq(8, 128), per_seq(1, 128), per_seq(1, 128), per_seq(8, 1),
                  pl.BlockSpec((PAGE, PAGE), lambda b, pt: (0, 0)), hbm, hbm, hbm],
        out_specs=pl.BlockSpec((None, 1, 256), lambda b, pt: (b, 0, 0)),
        scratch_shapes=[pltpu.VMEM((2, pg, 2, HEAD_DIM, PAGE), F32), pltpu.VMEM((2, pg, 2, HEAD_DIM, PAGE), F32),
                        pltpu.VMEM((2, pg, N_HEADS, PAGE), F32), pltpu.SemaphoreType.DMA((2, 3)),
                        pltpu.VMEM((8, 1), F32), pltpu.VMEM((8, 1), F32), pltpu.VMEM((8, 128), F32),
                        pltpu.VMEM((8, 1), F32), pltpu.VMEM((pg * 8, PAGE), F32)])
    out = pl.pallas_call(
        functools.partial(_fox_decode_kernel, layer=layer, pg=pg, n_pages=n_pages),
        grid_spec=grid_spec,
        out_shape=jax.ShapeDtypeStruct((nb, 1, 256), F32),
        compiler_params=_cparams(("arbitrary",)),
        name="fox_paged_attention",
    )(page_table, q8, k_new, v_new, cn8, slt, cache_k, cache_v, cache_lf)
    return out.reshape(nb, 256)


def _merge_kernel(x_ref, pg_ref, ya_ref, yb_ref, yc_ref, yd_ref, wb_ref, wo_ref, o_ref):
    pg = pg_ref[...]
    merged = None
    for bi, y_ref in enumerate((ya_ref, yb_ref, yc_ref, yd_ref)):
        gate = _sigmoid(pg[:, bi * D_MODEL:(bi + 1) * D_MODEL])
        term = gate * jnp.dot(y_ref[...].astype(BF16), wb_ref[bi], preferred_element_type=F32)
        merged = term if merged is None else merged + term
    o_ref[...] = x_ref[...] + jnp.dot(merged.astype(BF16), wo_ref[...], preferred_element_type=F32)


def _merge(x, p, ya, yb, yc, yd, wb, wo, tm):
    rows = x.shape[0]
    row = lambda w: pl.BlockSpec((tm, w), lambda i: (i, 0))
    return pl.pallas_call(
        _merge_kernel,
        grid=(rows // tm,),
        in_specs=[row(D_MODEL), pl.BlockSpec((tm, 4 * D_MODEL), lambda i: (i, P_GATE)),
                  row(256), row(256), row(256), row(256),
                  pl.BlockSpec((4, BRANCH_W, D_MODEL), lambda i: (0, 0, 0)),
                  pl.BlockSpec((D_MODEL, D_MODEL), lambda i: (0, 0))],
        out_specs=row(D_MODEL),
        out_shape=jax.ShapeDtypeStruct((rows, D_MODEL), F32),
        compiler_params=_cparams(("parallel",)),
        name="gated_merge",
    )(x, p, ya, yb, yc, yd, wb, wo)


def _mlp_kernel(x_ref, g_ref, wu_ref, wd_ref, fg_ref, *o_refs, final):
    x = x_ref[...]
    h = _rms(x, g_ref[...]).astype(BF16)
    u = jnp.maximum(jnp.dot(h, wu_ref[...], preferred_element_type=F32), 0.0)
    y = x + jnp.dot((u * u).astype(BF16), wd_ref[...], preferred_element_type=F32)
    o_refs[0][...] = y
    if final:
        o_refs[1][...] = _rms(y, fg_ref[...])


def _mlp(x, g, wu, wd, fg, tm, final):
    rows = x.shape[0]
    row = pl.BlockSpec((tm, D_MODEL), lambda i: (i, 0))
    vec = pl.BlockSpec((1, D_MODEL), lambda i: (0, 0))
    const = lambda a, b: pl.BlockSpec((a, b), lambda i: (0, 0), pipeline_mode=pl.Buffered(1))
    n_out = 2 if final else 1
    return pl.pallas_call(
        functools.partial(_mlp_kernel, final=final),
        grid=(rows // tm,),
        in_specs=[row, vec, const(D_MODEL, D_FF), const(D_FF, D_MODEL), vec],
        out_specs=[row] * n_out,
        out_shape=[jax.ShapeDtypeStruct((rows, D_MODEL), F32)] * n_out,
        compiler_params=_cparams(("parallel",)),
        name="mlp",
    )(x, g, wu, wd, fg)


def _layer(x, lw, consts, *, nseq, seq_len, tm, decode):
    rows = x.shape[0]
    p = _norm_matmul(x, lw["norm1_g"], lw["w_in"], tm)
    prompt = decode is None

    if prompt:
        r, w, kh, v, kk, b, g = _rwkv_prep(p, None, lw, consts["bd"], tm, True, seq_len)
        y, rwkv_new = _rwkv_scan(r, w, kh, v, kk, b, nseq, seq_len)
    else:
        r, w, kh, v, kk, b, g = _rwkv_prep(p, decode["shift"], lw, consts["bd"], tm, False, seq_len)
        y, rwkv_new = _rwkv_single(r, w, kh, v, kk, b, decode["rwkv"])
    ya = _rwkv_post(y, r, kh, v, g, lw, tm)
    shift_new = p[:, P_RWKV:P_RWKV + RWKV_IN].reshape(nseq, seq_len, RWKV_IN)[:, -1]

    if prompt:
        yb, ret_new = _ret_prompt(p, nseq, seq_len)
    else:
        yb, ret_new = _ret_single(p, decode["ret"], decode["pos"])

    pos = jnp.arange(seq_len) if prompt else jnp.full((1,), decode["pos"])
    qf, mla_rows = _mla_prep(p, lw, pos, tm, max(seq_len // tm, 1))
    if prompt:
        yc = _mla_flash(qf, mla_rows, lw["wuv"], nseq, seq_len, min(ATTN_TILE, seq_len))
    else:
        q8 = jnp.pad(jnp.transpose(qf, (1, 0, 2)), ((0, 0), (0, 8 - N_HEADS), (0, 0)))
        yc = _mla_decode(decode["cache_mla"], decode["layer"], decode["page_table"], q8,
                         mla_rows.reshape(rows, 1, MLA_WP), lw["wuv"], decode["pg_mla"])

    fq, fk, fv, lf, c, ct = _fox_prep(p, lw["bf"], tm, nseq, seq_len, prompt)
    if prompt:
        yd = _fox_flash(fq, fk, fv, c, ct, nseq, seq_len, min(ATTN_TILE, seq_len))
    else:
        q8 = jnp.zeros((rows, 8, 128), F32)
        for h in range(N_HEADS):
            gq = (h // 2) * HEAD_DIM
            q8 = q8.at[:, h, gq:gq + HEAD_DIM].set(fq[h])
        cn8 = jnp.pad(lf[:, 0:N_HEADS], ((0, 0), (0, 8 - N_HEADS))).reshape(rows, 8, 1)
        k_new = p[:, P_FK:P_FK + 128].reshape(rows, 1, 128)
        v_new = p[:, P_FV:P_FV + 128].reshape(rows, 1, 128)
        yd = _fox_decode(decode["cache_k"], decode["cache_v"], decode["cache_lf"], decode["layer"],
                         decode["page_table"], q8, k_new, v_new, cn8, decode["pg_fox"])

    x1 = _merge(x, p, ya, yb, yc, yd, lw["w_branch"], lw["w_out"], tm)
    new = (mla_rows[:, 0:MLA_W], p[:, P_FK:P_FK + 128], p[:, P_FV:P_FV + 128], lf[:, 0:N_HEADS],
           rwkv_new, shift_new, ret_new)
    return x1, new


def _layer_weights(l, norm1_g, norm2_g, w_in, rwkv_mu, rwkv_w0, rwkv_w2, rwkv_a0, rwkv_a2, rwkv_g2, rwkv_kk,
                   rwkv_ka, rwkv_rk, rwkv_ln_g, rwkv_ln_b, mla_qn_g, mla_kvn_g, mla_wuq, mla_wuk, mla_wuv,
                   fox_bf, w_branch, w_out, w_up, w_down):
    wi = w_in[l]
    a, b = wi[:, 0:1024], wi[:, 1024:2048]
    cq, ckv, kr = wi[:, 2048:2240], wi[:, 2240:2496], wi[:, 2496:2528]
    fq, fk, fv, fl = wi[:, 2528:2784], wi[:, 2784:2912], wi[:, 2912:3040], wi[:, 3040:3044]
    gate = wi[:, 3044:7140]
    z = lambda n: jnp.zeros((D_MODEL, n), F32)
    w_all = jnp.concatenate([gate, a, b, fq, fk, fv, cq, z(256 - Q_LORA), ckv, kr, fl, z(128 - ROPE_DIM - 4)],
                            axis=1).astype(BF16)
    row = lambda v: v.reshape(1, -1)
    wuq = jnp.pad(mla_wuq[l], ((0, 256 - Q_LORA), (0, 0), (0, 0)))
    wuk_bd = jnp.zeros((N_HEADS * NOPE_DIM, N_HEADS * KV_LORA), F32)
    for h in range(N_HEADS):
        wuk_bd = wuk_bd.at[h * NOPE_DIM:(h + 1) * NOPE_DIM, h * KV_LORA:(h + 1) * KV_LORA].set(mla_wuk[l][:, h, :].T)
    return dict(
        norm1_g=row(norm1_g[l]), norm2_g=row(norm2_g[l]), w_in=w_all,
        mu=row(rwkv_mu[l]), w0=row(rwkv_w0[l]), w2=rwkv_w2[l].astype(BF16), a0=row(rwkv_a0[l]),
        a2=rwkv_a2[l].astype(BF16), g2=rwkv_g2[l].astype(BF16), kkp=row(rwkv_kk[l]), ka=row(rwkv_ka[l]),
        rk=rwkv_rk[l].reshape(N_HEADS, 1, HEAD_DIM), ln_g=row(rwkv_ln_g[l]), ln_b=row(rwkv_ln_b[l]),
        qn_g=row(jnp.pad(mla_qn_g[l], (0, 256 - Q_LORA))), kvn_g=row(mla_kvn_g[l]),
        w_nope=wuq[:, :, 0:NOPE_DIM].reshape(256, N_HEADS * NOPE_DIM).astype(BF16),
        w_rope=wuq[:, :, NOPE_DIM:].reshape(256, N_HEADS * ROPE_DIM).astype(BF16),
        wuk_bd=wuk_bd.astype(BF16),
        wuv=jnp.transpose(mla_wuv[l], (1, 0, 2)).astype(BF16),
        bf=jnp.pad(fox_bf[l], (ROPE_DIM, 128 - ROPE_DIM - 4)).reshape(1, 128),
        w_branch=w_branch[l].astype(BF16), w_out=w_out[l].astype(BF16),
        w_up=w_up[l].astype(BF16), w_down=w_down[l].astype(BF16))


def kernel(x_prompt, x_sample, cache_mla, cache_fox_k, cache_fox_v, cache_fox_logf, state_rwkv, state_rwkv_shift, state_ret, page_table, norm1_g, norm2_g, final_g, w_in, rwkv_mu, rwkv_w0, rwkv_w2, rwkv_a0, rwkv_a2, rwkv_g2, rwkv_kk, rwkv_ka, rwkv_rk, rwkv_ln_g, rwkv_ln_b, mla_qn_g, mla_kvn_g, mla_wuq, mla_wuk, mla_wuv, fox_bf, w_branch, w_out, w_up, w_down):
    nseq, seq_len = x_prompt.shape[:2]
    nb, dec_len = x_sample.shape[:2]
    assert dec_len == 1
    depth = w_in.shape[0]
    n_pages = page_table.shape[1]
    t_past = n_pages * PAGE
    n_pool = cache_mla.shape[1]
    pg_mla = min(32, n_pages)
    pg_fox = min(32, n_pages)
    tm_p = min(512, seq_len)
    tm_s = nb

    cache_mla_t = jnp.transpose(cache_mla, (0, 1, 3, 2))
    cache_k_t = jnp.transpose(cache_fox_k, (0, 1, 3, 4, 2))
    cache_v_t = jnp.transpose(cache_fox_v, (0, 1, 3, 4, 2))
    cache_lf_t = jnp.transpose(cache_fox_logf, (0, 1, 3, 2))
    consts = dict(bd=(jnp.arange(256)[:, None] // HEAD_DIM == jnp.arange(256)[None, :] // HEAD_DIM).astype(BF16))
    fg = final_g.reshape(1, D_MODEL)

    xp = x_prompt.reshape(nseq * seq_len, D_MODEL)
    xs = x_sample.reshape(nb, D_MODEL)
    new_p, new_s = [], []
    yp = ys = None
    for l in range(depth):
        lw = _layer_weights(l, norm1_g, norm2_g, w_in, rwkv_mu, rwkv_w0, rwkv_w2, rwkv_a0, rwkv_a2, rwkv_g2,
                            rwkv_kk, rwkv_ka, rwkv_rk, rwkv_ln_g, rwkv_ln_b, mla_qn_g, mla_kvn_g, mla_wuq,
                            mla_wuk, mla_wuv, fox_bf, w_branch, w_out, w_up, w_down)
        final = l == depth - 1
        x1, st = _layer(xp, lw, consts, nseq=nseq, seq_len=seq_len, tm=tm_p, decode=None)
        outs = _mlp(x1, lw["norm2_g"], lw["w_up"], lw["w_down"], fg, tm_p, final)
        xp = outs[0]
        if final:
            yp = outs[1]
        new_p.append(st)
        decode = dict(shift=state_rwkv_shift[l], rwkv=state_rwkv[l], ret=state_ret[l], pos=t_past, layer=l,
                      page_table=page_table, cache_mla=cache_mla_t, cache_k=cache_k_t, cache_v=cache_v_t,
                      cache_lf=cache_lf_t, pg_mla=pg_mla, pg_fox=pg_fox)
        x1, st = _layer(xs, lw, consts, nseq=nb, seq_len=1, tm=tm_s, decode=decode)
        outs = _mlp(x1, lw["norm2_g"], lw["w_up"], lw["w_down"], fg, tm_s, final)
        xs = outs[0]
        if final:
            ys = outs[1]
        new_s.append(st)

    def stack(new, i, shape):
        return jnp.stack([st[i] for st in new]).reshape((depth,) + shape)

    res = [yp.reshape(nseq, seq_len, D_MODEL), ys.reshape(nb, 1, D_MODEL)]
    shapes_p = [(nseq, seq_len, MLA_W), (nseq, seq_len, 2, HEAD_DIM), (nseq, seq_len, 2, HEAD_DIM),
                (nseq, seq_len, N_HEADS), (nseq, N_HEADS, HEAD_DIM, HEAD_DIM), (nseq, RWKV_IN),
                (nseq, N_HEADS, HEAD_DIM, HEAD_DIM)]
    shapes_s = [(nb, 1, MLA_W), (nb, 1, 2, HEAD_DIM), (nb, 1, 2, HEAD_DIM), (nb, 1, N_HEADS),
                (nb, N_HEADS, HEAD_DIM, HEAD_DIM), (nb, RWKV_IN), (nb, N_HEADS, HEAD_DIM, HEAD_DIM)]
    for i in range(7):
        res.append(stack(new_p, i, shapes_p[i]))
        res.append(stack(new_s, i, shapes_s[i]))
    return tuple(res)
```
